```python
import math
import jax
import jax.numpy as jnp
from jax import lax
import numpy as np

D_MODEL = 4096
BATCH = 4
SEQ = 2048
DEPTH = 2

N_META = 16
GRID_W = 64
Q_BLOCK = 128
EPS = 1e-6

N_BRANCHES = 4
BRANCH_W = D_MODEL // N_BRANCHES
MIX_W = N_BRANCHES * BRANCH_W
HEAD_DIM = 128

A_HEADS = BRANCH_W // HEAD_DIM
A_KV_HEADS = max(1, A_HEADS // 4)
A_GROUPS = A_HEADS // A_KV_HEADS
AXIAL_THETA = 10000.0

HYENA_ORDER = 2
HYENA_CH = BRANCH_W
HYENA_EMB_DIM = 33
HYENA_FFN = 64
HYENA_SHORT = 3
HYENA_FAST = 0.3
HYENA_SLOW = 1.5
HYENA_TARGET = 1e-2
HYENA_SHIFT = 0.05

POOL_WINDOWS = (2, 4, 8, 16)
POOL_GROUPS = len(POOL_WINDOWS)
POOL_GROUP_W = BRANCH_W // POOL_GROUPS

D_HEAD_QK = HEAD_DIM
D_HEADS = BRANCH_W // (2 * D_HEAD_QK)
D_HEAD_V = 2 * D_HEAD_QK
ROPE_THETA = 500000.0
ROPE_DIMS = D_HEAD_QK // 4

A_Q_W = A_HEADS * HEAD_DIM
A_KV_W = A_KV_HEADS * HEAD_DIM
B_W = (HYENA_ORDER + 1) * HYENA_CH
C_W = BRANCH_W
D_QK_W = D_HEADS * 2 * D_HEAD_QK
D_V_W = D_HEADS * D_HEAD_V
GATE_W = MIX_W
SPLITS = (A_Q_W, A_KV_W, A_KV_W, B_W, C_W, D_QK_W, D_QK_W, D_V_W, GATE_W)
IN_COLS = A_Q_W + 2 * A_KV_W + B_W + C_W + 2 * D_QK_W + D_V_W + GATE_W

kernel_name = 'hymba_style_parallel_hybrid_encoder'


def split_points():
    pts, acc = [], 0
    for s in SPLITS[:-1]:
        acc += s
        pts.append(acc)
    return pts


def rms_norm(x, w):
    xf = x.astype(jnp.float32)
    y = xf * lax.rsqrt(jnp.mean(xf * xf, axis=-1, keepdims=True) + EPS)
    return (y * w.astype(jnp.float32)).astype(x.dtype)


def inv_freq(dim, theta):
    return theta ** (-jnp.arange(0, dim, 2, dtype=jnp.float32) / dim)


def rope_rotate(x, ang):
    m = ang.shape[-1]
    shape = (1, ang.shape[0]) + (1,) * (x.ndim - 3) + (m,)
    c = jnp.cos(ang).reshape(shape)
    s = jnp.sin(ang).reshape(shape)
    xf = x.astype(jnp.float32)
    x1, x2 = xf[..., :m], xf[..., m:]
    return jnp.concatenate([x1 * c - x2 * s, x2 * c + x1 * s], axis=-1).astype(x.dtype)


def sweep_query_blocks(block_fn, q):
    q_meta, q_real = q[..., :N_META, :], q[..., N_META:, :]
    nb = q_real.shape[-2] // Q_BLOCK
    qb = q_real.reshape(q_real.shape[:-2] + (nb, Q_BLOCK, q_real.shape[-1]))
    qb = jnp.moveaxis(qb, -3, 0)
    ob = jnp.moveaxis(lax.map(block_fn, qb), 0, -3)
    ob = ob.reshape(ob.shape[:-3] + (nb * Q_BLOCK, ob.shape[-1]))
    return jnp.concatenate([block_fn(q_meta), ob], axis=-2)


def gqa_axial_mixer(aq, ak, av, q_norm, k_norm, ang_row, ang_col):
    B, L, _ = aq.shape
    q = rms_norm(aq.reshape(B, L, A_HEADS, HEAD_DIM), q_norm)
    k = rms_norm(ak.reshape(B, L, A_KV_HEADS, HEAD_DIM), k_norm)
    v = av.reshape(B, L, A_KV_HEADS, HEAD_DIM)
    half = HEAD_DIM // 2

    def axial(t):
        return jnp.concatenate([rope_rotate(t[..., :half], ang_row),
                                rope_rotate(t[..., half:], ang_col)], axis=-1)

    q, k = axial(q), axial(k)
    q = q.reshape(B, L, A_KV_HEADS, A_GROUPS, HEAD_DIM).transpose(0, 2, 3, 1, 4)
    k = k.transpose(0, 2, 1, 3)
    v = v.transpose(0, 2, 1, 3)
    scale = HEAD_DIM ** -0.5

    def block(qb):
        s = jnp.einsum('bkgqd,bksd->bkgqs', qb, k).astype(jnp.float32) * scale
        p = jax.nn.softmax(s, axis=-1).astype(v.dtype)
        return jnp.einsum('bkgqs,bksd->bkgqd', p, v)

    o = sweep_query_blocks(block, q)
    return o.transpose(0, 3, 1, 2, 4).reshape(B, L, A_HEADS * HEAD_DIM)


def short_conv3(u, w):
    L = u.shape[1]
    up = jnp.pad(u, ((0, 0), (1, 1), (0, 0)))
    return up[:, :L] * w[0] + up[:, 1:L + 1] * w[1] + up[:, 2:L + 2] * w[2]


def hyena_filter_spectra(L, w1, b1, w2, b2, w3, freq):
    t = jnp.linspace(0.0, 1.0, L, dtype=jnp.float32)[:, None]
    bands = (HYENA_EMB_DIM - 1) // 2
    fb = jnp.linspace(1e-4, bands - 1, bands, dtype=jnp.float32)[None, :]
    wpos = 2.0 * math.pi * jnp.arange(L, dtype=jnp.float32)[:, None] / L
    z = jnp.concatenate([t, jnp.cos(fb * wpos), -jnp.sin(fb * wpos)], axis=-1)
    fr = freq.astype(jnp.float32)
    hdn = jnp.sin(fr * (z @ w1.astype(jnp.float32) + b1.astype(jnp.float32)))
    hdn = jnp.sin(fr * (hdn @ w2.astype(jnp.float32) + b2.astype(jnp.float32)))
    h = (hdn @ w3.astype(jnp.float32)).reshape(L, HYENA_ORDER, 2, HYENA_CH)
    deltas = jnp.abs(jnp.linspace(math.log(HYENA_TARGET) / HYENA_SLOW,
                                  math.log(HYENA_TARGET) / HYENA_FAST, HYENA_CH, dtype=jnp.float32))
    h = h * (jnp.exp(-t * deltas) + HYENA_SHIFT)[:, None, None, :]
    h_two = jnp.concatenate([h[:, :, 0],
                             jnp.zeros((1, HYENA_ORDER, HYENA_CH), jnp.float32),
                             h[1:, :, 1][::-1]], axis=0)
    return jnp.fft.rfft(h_two, axis=0)


def hyena_mixer(bvx, conv_w, w1, b1, w2, b2, w3, freq, skip):
    B, L, _ = bvx.shape
    u = short_conv3(bvx, conv_w)
    v, x1, x2 = jnp.split(u, 3, axis=-1)
    hf = hyena_filter_spectra(L, w1, b1, w2, b2, w3, freq)
    z = v
    for o, gate in enumerate((x1, x2)):
        zf = z.astype(jnp.float32)
        conv = jnp.fft.irfft(jnp.fft.rfft(zf, n=2 * L, axis=1) * hf[None, :, o], n=2 * L, axis=1)[:, :L]
        z = (gate.astype(jnp.float32) * (conv + skip[o].astype(jnp.float32) * zf)).astype(bvx.dtype)
    return z


def pool_mixer(p, group_w, scale):
    B, L, C = p.shape
    pf = p.astype(jnp.float32)
    cs = jnp.concatenate([jnp.zeros((B, 1, C), jnp.float32), jnp.cumsum(pf, axis=1)], axis=1)
    t = jnp.arange(L)
    outs = []
    for g, w in enumerate(POOL_WINDOWS):
        lo = jnp.clip(t - (w - 1) // 2, 0, L)
        hi = jnp.clip(t + w // 2 + 1, 0, L)
        sl = slice(g * POOL_GROUP_W, (g + 1) * POOL_GROUP_W)
        csg = cs[..., sl]
        cnt = (hi - lo).astype(jnp.float32)[None, :, None]
        outs.append((csg[:, hi] - csg[:, lo]) / cnt - pf[..., sl])
    d = jnp.stack(outs, axis=2).astype(p.dtype)
    y = jnp.einsum('blgc,gce->blge', d, group_w)
    return y.reshape(B, L, C) * scale


def diff_attention_mixer(dq, dk, dv, q_norm, k_norm, lam_vec, out_norm, lam_init, ang):
    B, L, _ = dq.shape

    def prep(t, g):
        t = rms_norm(t.reshape(B, L, D_HEADS, 2, D_HEAD_QK), g)
        t = jnp.concatenate([rope_rotate(t[..., :ROPE_DIMS], ang), t[..., ROPE_DIMS:]], axis=-1)
        return t.reshape(B, L, D_HEADS, 2 * D_HEAD_QK).transpose(0, 2, 1, 3)

    q, k = prep(dq, q_norm), prep(dk, k_norm)
    v = dv.reshape(B, L, D_HEADS, D_HEAD_V).transpose(0, 2, 1, 3)
    lf = lam_vec.astype(jnp.float32)
    lam = jnp.exp(jnp.sum(lf[0] * lf[1])) - jnp.exp(jnp.sum(lf[2] * lf[3])) + lam_init
    k1, k2 = k[..., :D_HEAD_QK], k[..., D_HEAD_QK:]
    scale = D_HEAD_QK ** -0.5

    def block(qb):
        s1 = jnp.einsum('bhqd,bhsd->bhqs', qb[..., :D_HEAD_QK], k1).astype(jnp.float32) * scale
        s2 = jnp.einsum('bhqd,bhsd->bhqs', qb[..., D_HEAD_QK:], k2).astype(jnp.float32) * scale
        a = jax.nn.softmax(s1, axis=-1) - lam * jax.nn.softmax(s2, axis=-1)
        return jnp.einsum('bhqs,bhsd->bhqd', a.astype(v.dtype), v)

    o = sweep_query_blocks(block, q)
    o = rms_norm(o, out_norm) * (1.0 - lam_init)
    return o.transpose(0, 2, 1, 3).reshape(B, L, D_HEADS * D_HEAD_V)


def setup_inputs(seed: int = 0) -> dict:
    key = jax.random.key(seed)
    ks = jax.random.split(key, 24)

    def nrm(k, shape, s):
        return jax.random.normal(k, shape, jnp.float32) * s

    def gain(k, shape):
        return 1.0 + 0.02 * jax.random.normal(k, shape, jnp.float32)

    return {
        'x': nrm(ks[0], (BATCH, SEQ, D_MODEL), 1.0),
        'meta_tokens': nrm(ks[1], (N_META, D_MODEL), 1.0),
        'norm_w': gain(ks[2], (DEPTH, D_MODEL)),
        'w_in': nrm(ks[3], (DEPTH, D_MODEL, IN_COLS), D_MODEL ** -0.5),
        'w_out': nrm(ks[4], (DEPTH, MIX_W, D_MODEL), MIX_W ** -0.5),
        'a_q_norm': gain(ks[5], (DEPTH, HEAD_DIM)),
        'a_k_norm': gain(ks[6], (DEPTH, HEAD_DIM)),
        'a_out_norm': gain(ks[7], (DEPTH, BRANCH_W)),
        'b_short_conv': nrm(ks[8], (DEPTH, HYENA_SHORT, B_W), HYENA_SHORT ** -0.5),
        'b_filt_w1': nrm(ks[9], (DEPTH, HYENA_EMB_DIM, HYENA_FFN), HYENA_EMB_DIM ** -0.5),
        'b_filt_b1': nrm(ks[10], (DEPTH, HYENA_FFN), 0.1),
        'b_filt_w2': nrm(ks[11], (DEPTH, HYENA_FFN, HYENA_FFN), HYENA_FFN ** -0.5),
        'b_filt_b2': nrm(ks[12], (DEPTH, HYENA_FFN), 0.1),
        'b_filt_w3': nrm(ks[13], (DEPTH, HYENA_FFN, HYENA_ORDER * 2 * HYENA_CH), 0.1 * HYENA_FFN ** -0.5),
        'b_sin_freq': gain(ks[14], (DEPTH, HYENA_FFN)),
        'b_skip': nrm(ks[15], (DEPTH, HYENA_ORDER, HYENA_CH), 0.5),
        'b_out_norm': gain(ks[16], (DEPTH, BRANCH_W)),
        'c_group_w': nrm(ks[17], (DEPTH, POOL_GROUPS, POOL_GROUP_W, POOL_GROUP_W), POOL_GROUP_W ** -0.5),
        'c_scale': gain(ks[18], (DEPTH, BRANCH_W)),
        'd_q_norm': gain(ks[19], (DEPTH, D_HEAD_QK)),
        'd_k_norm': gain(ks[20], (DEPTH, D_HEAD_QK)),
        'd_lambda': nrm(ks[21], (DEPTH, 4, D_HEAD_QK), 0.1),
        'd_out_norm': gain(ks[22], (DEPTH, D_HEAD_V)),
    }


def reference(x, meta_tokens, norm_w, w_in, w_out, a_q_norm, a_k_norm, a_out_norm,
              b_short_conv, b_filt_w1, b_filt_b1, b_filt_w2, b_filt_b2, b_filt_w3,
              b_sin_freq, b_skip, b_out_norm, c_group_w, c_scale,
              d_q_norm, d_k_norm, d_lambda, d_out_norm):
    B = x.shape[0]
    meta = jnp.broadcast_to(meta_tokens.astype(x.dtype)[None], (B, N_META, x.shape[-1]))
    h = jnp.concatenate([meta, x], axis=1)
    L = h.shape[1]
    n_real = L - N_META
    rows = n_real // GRID_W

    zeros_meta = jnp.zeros((N_META,), jnp.float32)
    row = jnp.concatenate([zeros_meta, jnp.repeat(jnp.arange(rows, dtype=jnp.float32), GRID_W)])
    col = jnp.concatenate([zeros_meta, jnp.tile(jnp.arange(GRID_W, dtype=jnp.float32), rows)])
    ax_freq = inv_freq(HEAD_DIM // 2, AXIAL_THETA)
    ang_row = row[:, None] * ax_freq[None, :]
    ang_col = col[:, None] * ax_freq[None, :]
    ang_1d = jnp.arange(L, dtype=jnp.float32)[:, None] * inv_freq(ROPE_DIMS, ROPE_THETA)[None, :]
    pts = split_points()

    for l in range(DEPTH):
        lam_init = 0.8 - 0.6 * math.exp(-0.3 * l)
        u = rms_norm(h, norm_w[l])
        proj = u @ w_in[l]
        aq, ak, av, bvx, cp, dq, dk, dv, gate = jnp.split(proj, pts, axis=-1)
        ya = rms_norm(gqa_axial_mixer(aq, ak, av, a_q_norm[l], a_k_norm[l], ang_row, ang_col), a_out_norm[l])
        yb = rms_norm(hyena_mixer(bvx, b_short_conv[l], b_filt_w1[l], b_filt_b1[l], b_filt_w2[l],
                                  b_filt_b2[l], b_filt_w3[l], b_sin_freq[l], b_skip[l]), b_out_norm[l])
        yc = pool_mixer(cp, c_group_w[l], c_scale[l])
        yd = diff_attention_mixer(dq, dk, dv, d_q_norm[l], d_k_norm[l], d_lambda[l], d_out_norm[l],
                                  lam_init, ang_1d)
        y = jnp.concatenate([ya, yb, yc, yd], axis=-1) * jax.nn.silu(gate)
        h = h + y @ w_out[l]

    return h[:, N_META:]
```

```python
import functools
import math

import jax
import jax.numpy as jnp
from jax import lax
from jax.experimental import pallas as pl
from jax.experimental.pallas import tpu as pltpu

F32 = jnp.float32
BF16 = jnp.bfloat16

D_MODEL = 4096
N_META = 16
GRID_W = 64
EPS = 1e-6
BRANCH_W = 1024
HEAD_DIM = 128
A_HEADS = 8
A_KV_HEADS = 2
A_GROUPS = A_HEADS // A_KV_HEADS
AXIAL_THETA = 10000.0
HYENA_EMB_DIM = 33
HYENA_FFN = 64
HYENA_FAST = 0.3
HYENA_SLOW = 1.5
HYENA_TARGET = 1e-2
HYENA_SHIFT = 0.05
POOL_WINDOWS = (2, 4, 8, 16)
POOL_GROUP_W = 256
D_HEADS = 4
D_HEAD_V = 256
ROPE_THETA = 500000.0
ROPE_DIMS = 32

COL_AQ = 0
COL_AK = 1024
COL_AV = 1280
COL_B = 1536
COL_C = 4608
COL_DQ = 5632
COL_DK = 6656
COL_DV = 7680
COL_GATE = 8704
MAIN_W = COL_GATE
IN_COLS = 12800

LANE = 128
SEQ_REAL = 2048
L_TRUE = N_META + SEQ_REAL
L_PAD = 2176
DFT_N = 4607
N_FREQ = 2304
FREQ_CHUNK = 384
NEG_BIG = -1e30
VMEM_CAP = 56 * 1024 * 1024


def _cparams(semantics, vmem_bytes):
    return pltpu.CompilerParams(dimension_semantics=semantics,
                                vmem_limit_bytes=min(int(vmem_bytes), VMEM_CAP))


def _rmsnorm_kernel(h_ref, w_ref, o_ref):
    x = h_ref[...]
    ms = jnp.mean(x * x, axis=-1, keepdims=True)
    o_ref[...] = (x * lax.rsqrt(ms + EPS) * w_ref[...]).astype(o_ref.dtype)


def _rmsnorm(h, w, tm=272):
    m, d = h.shape
    return pl.pallas_call(
        _rmsnorm_kernel,
        grid=(m // tm,),
        in_specs=[pl.BlockSpec((tm, d), lambda i: (i, 0)),
                  pl.BlockSpec((1, d), lambda i: (0, 0))],
        out_specs=pl.BlockSpec((tm, d), lambda i: (i, 0)),
        out_shape=jax.ShapeDtypeStruct((m, d), BF16),
        compiler_params=_cparams(("parallel",), 32 << 20),
    )(h, w.reshape(1, d))


def _matmul_kernel(x_ref, w_ref, o_ref, wb_ref):
    @pl.when(pl.program_id(1) == 0)
    def _():
        wb_ref[...] = w_ref[...].astype(BF16)

    o_ref[...] = jnp.dot(x_ref[...], wb_ref[...],
                         preferred_element_type=F32).astype(o_ref.dtype)


def _matmul_res_kernel(x_ref, w_ref, r_ref, o_ref, wb_ref):
    @pl.when(pl.program_id(1) == 0)
    def _():
        wb_ref[...] = w_ref[...].astype(BF16)

    o_ref[...] = r_ref[...] + jnp.dot(x_ref[...], wb_ref[...], preferred_element_type=F32)


def _matmul(x, w3, layer, col0, width, out_dtype, residual=None, tm=1088, tn=512):
    m, k = x.shape
    cb0 = col0 // tn
    grid = (width // tn, m // tm)
    in_specs = [pl.BlockSpec((tm, k), lambda j, i: (i, 0)),
                pl.BlockSpec((None, k, tn), lambda j, i: (layer, 0, cb0 + j))]
    args = [x, w3]
    kern = _matmul_kernel
    if residual is not None:
        in_specs.append(pl.BlockSpec((tm, tn), lambda j, i: (i, j)))
        args.append(residual)
        kern = _matmul_res_kernel
    vmem = 2 * (tm * k * 2 + k * tn * 4 + 2 * tm * tn * 4) + k * tn * 2 + (4 << 20)
    return pl.pallas_call(
        kern,
        grid=grid,
        in_specs=in_specs,
        out_specs=pl.BlockSpec((tm, tn), lambda j, i: (i, j)),
        out_shape=jax.ShapeDtypeStruct((m, width), out_dtype),
        scratch_shapes=[pltpu.VMEM((k, tn), BF16)],
        compiler_params=_cparams(("parallel", "arbitrary"), vmem),
    )(*args)


def _norm_rope(x, gain, cos, sin_signed, half_rot):
    ms = jnp.mean(x * x, axis=-1, keepdims=True)
    xn = x * lax.rsqrt(ms + EPS) * gain
    lane = lax.broadcasted_iota(jnp.int32, (1, LANE), 1)
    partner_up = (lane & half_rot) == 0
    partner = jnp.where(partner_up,
                        pltpu.roll(xn, LANE - half_rot, 1),
                        pltpu.roll(xn, half_rot, 1))
    return xn * cos + partner * sin_signed


def _softmax_pv(q_bf, k_bf, v_bf, bias):
    s = lax.dot_general(q_bf, k_bf, (((1,), (1,)), ((), ())), preferred_element_type=F32)
    s = s + bias
    m = jnp.max(s, axis=-1, keepdims=True)
    p = jnp.exp(s - m)
    l = jnp.sum(p, axis=-1, keepdims=True)
    o = jnp.dot(p.astype(BF16), v_bf, preferred_element_type=F32)
    return o * (1.0 / l)


def _attn_a_kernel(q_ref, k_ref, v_ref, cq_ref, sq_ref, ck_ref, sk_ref, gq_ref, gk_ref,
                   bias_ref, o_ref, kh_ref):
    @pl.when((pl.program_id(2) == 0) & (pl.program_id(3) == 0))
    def _():
        kh_ref[...] = _norm_rope(k_ref[...].astype(F32), gk_ref[...], ck_ref[...], sk_ref[...],
                                 HEAD_DIM // 4).astype(BF16)

    q = _norm_rope(q_ref[...].astype(F32), gq_ref[...], cq_ref[...], sq_ref[...], HEAD_DIM // 4)
    q = (q * (HEAD_DIM ** -0.5)).astype(BF16)
    o_ref[...] = _softmax_pv(q, kh_ref[...], v_ref[...], bias_ref[...]).astype(o_ref.dtype)


def _attn_a(proj, cos_t, sin_t, gq, gk, bias, batch, tq=544):
    m = proj.shape[0]
    nq = L_PAD // tq
    qblk = lambda b, kv, g, qi: (b * nq + qi, COL_AQ // HEAD_DIM + kv * A_GROUPS + g)
    tabq = lambda b, kv, g, qi: (qi, 0)
    full = lambda b, kv, g, qi: (0, 0)
    return pl.pallas_call(
        _attn_a_kernel,
        grid=(batch, A_KV_HEADS, A_GROUPS, nq),
        in_specs=[
            pl.BlockSpec((tq, HEAD_DIM), qblk),
            pl.BlockSpec((L_PAD, HEAD_DIM), lambda b, kv, g, qi: (b, COL_AK // HEAD_DIM + kv)),
            pl.BlockSpec((L_PAD, HEAD_DIM), lambda b, kv, g, qi: (b, COL_AV // HEAD_DIM + kv)),
            pl.BlockSpec((tq, HEAD_DIM), tabq),
            pl.BlockSpec((tq, HEAD_DIM), tabq),
            pl.BlockSpec((L_PAD, HEAD_DIM), full),
            pl.BlockSpec((L_PAD, HEAD_DIM), full),
            pl.BlockSpec((1, HEAD_DIM), full),
            pl.BlockSpec((1, HEAD_DIM), full),
            pl.BlockSpec((1, L_PAD), full),
        ],
        out_specs=pl.BlockSpec((tq, HEAD_DIM), lambda b, kv, g, qi: (b * nq + qi, kv * A_GROUPS + g)),
        out_shape=jax.ShapeDtypeStruct((m, BRANCH_W), BF16),
        scratch_shapes=[pltpu.VMEM((L_PAD, HEAD_DIM), BF16)],
        compiler_params=_cparams(("parallel", "parallel", "arbitrary", "arbitrary"), 40 << 20),
    )(proj, proj, proj, cos_t, sin_t, cos_t, sin_t, gq.reshape(1, -1), gk.reshape(1, -1), bias)


def _attn_d_kernel(lam_init, q_ref, k_ref, v_ref, cq_ref, sq_ref, ck_ref, sk_ref, gq_ref, gk_ref,
                   lam_ref, gout_ref, bias_ref, o_ref, kh_ref):
    halves = (slice(0, HEAD_DIM), slice(HEAD_DIM, 2 * HEAD_DIM))

    @pl.when(pl.program_id(2) == 0)
    def _():
        for sl in halves:
            kh_ref[:, sl] = _norm_rope(k_ref[:, sl].astype(F32), gk_ref[...], ck_ref[...],
                                       sk_ref[...], ROPE_DIMS // 2).astype(BF16)

    lf = lam_ref[...]
    lam = (jnp.exp(jnp.sum(lf[0:1] * lf[1:2], axis=-1, keepdims=True))
           - jnp.exp(jnp.sum(lf[2:3] * lf[3:4], axis=-1, keepdims=True)) + lam_init)
    v = v_ref[...]
    bias = bias_ref[...]
    outs = []
    for sl in halves:
        q = _norm_rope(q_ref[:, sl].astype(F32), gq_ref[...], cq_ref[...], sq_ref[...],
                       ROPE_DIMS // 2)
        q = (q * (HEAD_DIM ** -0.5)).astype(BF16)
        outs.append(_softmax_pv(q, kh_ref[:, sl], v, bias))
    o = outs[0] - lam * outs[1]
    ms = jnp.mean(o * o, axis=-1, keepdims=True)
    o = o * lax.rsqrt(ms + EPS) * gout_ref[...] * (1.0 - lam_init)
    o_ref[...] = o.astype(o_ref.dtype)


def _attn_d(proj, cos_t, sin_t, gq, gk, lam_vec, gout, bias, lam_init, batch, tq=544):
    m = proj.shape[0]
    nq = L_PAD // tq
    w = 2 * HEAD_DIM
    tabq = lambda b, h, qi: (qi, 0)
    full = lambda b, h, qi: (0, 0)
    return pl.pallas_call(
        functools.partial(_attn_d_kernel, lam_init),
        grid=(batch, D_HEADS, nq),
        in_specs=[
            pl.BlockSpec((tq, w), lambda b, h, qi: (b * nq + qi, COL_DQ // w + h)),
            pl.BlockSpec((L_PAD, w), lambda b, h, qi: (b, COL_DK // w + h)),
            pl.BlockSpec((L_PAD, w), lambda b, h, qi: (b, COL_DV // w + h)),
            pl.BlockSpec((tq, HEAD_DIM), tabq),
            pl.BlockSpec((tq, HEAD_DIM), tabq),
            pl.BlockSpec((L_PAD, HEAD_DIM), full),
            pl.BlockSpec((L_PAD, HEAD_DIM), full),
            pl.BlockSpec((1, HEAD_DIM), full),
            pl.BlockSpec((1, HEAD_DIM), full),
            pl.BlockSpec((4, HEAD_DIM), full),
            pl.BlockSpec((1, w), full),
            pl.BlockSpec((1, L_PAD), full),
        ],
        out_specs=pl.BlockSpec((tq, w), lambda b, h, qi: (b * nq + qi, h)),
        out_shape=jax.ShapeDtypeStruct((m, BRANCH_W), BF16),
        scratch_shapes=[pltpu.VMEM((L_PAD, w), BF16)],
        compiler_params=_cparams(("parallel", "parallel", "arbitrary"), 48 << 20),
    )(proj, proj, proj, cos_t, sin_t, cos_t, sin_t, gq.reshape(1, -1), gk.reshape(1, -1),
      lam_vec, gout.reshape(1, -1), bias)


def _hyena_filter_kernel(z_ref, w1_ref, b1_ref, w2_ref, b2_ref, fr_ref, w3f_ref, w3b_ref,
                         dec_ref, o_ref, hid_ref):
    hp = lax.Precision.HIGHEST

    @pl.when((pl.program_id(0) == 0) & (pl.program_id(1) == 0))
    def _():
        fr = fr_ref[...]
        h1 = jnp.sin(fr * (jnp.dot(z_ref[...], w1_ref[...], precision=hp,
                                   preferred_element_type=F32) + b1_ref[...]))
        hid_ref[...] = jnp.sin(fr * (jnp.dot(h1, w2_ref[...], precision=hp,
                                             preferred_element_type=F32) + b2_ref[...]))

    hid = hid_ref[...]
    dec = dec_ref[...]
    lag = lax.broadcasted_iota(jnp.int32, (L_PAD, 1), 0)
    hf = jnp.dot(hid, w3f_ref[...], precision=hp, preferred_element_type=F32) * dec
    hb = jnp.dot(hid, w3b_ref[...], precision=hp, preferred_element_type=F32) * dec
    hf = jnp.where(lag < L_TRUE, hf, 0.0)
    hb = jnp.where((lag >= 1) & (lag < L_TRUE), hb, 0.0)
    o_ref[0] = (hf + hb).astype(o_ref.dtype)
    o_ref[1] = (hf - hb).astype(o_ref.dtype)


def _hyena_filter(zfeat, w1, b1, w2, b2, fr, w3, decay, tc=512):
    nct = BRANCH_W // tc
    full = lambda o, c: (0, 0)
    return pl.pallas_call(
        _hyena_filter_kernel,
        grid=(2, nct),
        in_specs=[
            pl.BlockSpec((L_PAD, LANE), full),
            pl.BlockSpec((LANE, LANE), full),
            pl.BlockSpec((1, LANE), full),
            pl.BlockSpec((LANE, LANE), full),
            pl.BlockSpec((1, LANE), full),
            pl.BlockSpec((1, LANE), full),
            pl.BlockSpec((LANE, tc), lambda o, c: (0, (o * 2) * nct + c)),
            pl.BlockSpec((LANE, tc), lambda o, c: (0, (o * 2 + 1) * nct + c)),
            pl.BlockSpec((L_PAD, tc), lambda o, c: (0, c)),
        ],
        out_specs=pl.BlockSpec((2, L_PAD, tc), lambda o, c: (0, 0, o * nct + c)),
        out_shape=jax.ShapeDtypeStruct((2, L_PAD, 2 * BRANCH_W), BF16),
        scratch_shapes=[pltpu.VMEM((L_PAD, LANE), F32)],
        compiler_params=_cparams(("arbitrary", "arbitrary"), 48 << 20),
    )(zfeat, w1, b1, w2, b2, fr, w3, w3, decay)


def _spectra_kernel(tf, f_ref, h_ref, o_ref):
    k = pl.program_id(1) * tf + lax.broadcasted_iota(jnp.int32, (tf, 1), 0)
    wk = jnp.where(k == 0, 1.0 / DFT_N, 2.0 / DFT_N)
    o_ref[...] = jnp.dot(f_ref[...], h_ref[...], preferred_element_type=F32) * wk


def _spectra(fmat, hsd, tf=768, tn=512):
    ncol = hsd.shape[-1]
    return pl.pallas_call(
        functools.partial(_spectra_kernel, tf),
        grid=(2, N_FREQ // tf, ncol // tn),
        in_specs=[pl.BlockSpec((None, tf, L_PAD), lambda p, f, n: (p, f, 0)),
                  pl.BlockSpec((None, L_PAD, tn), lambda p, f, n: (p, 0, n))],
        out_specs=pl.BlockSpec((None, tf, tn), lambda p, f, n: (p, f, n)),
        out_shape=jax.ShapeDtypeStruct((2, N_FREQ, ncol), F32),
        compiler_params=_cparams(("parallel", "parallel", "arbitrary"), 32 << 20),
    )(fmat, hsd)


def _short_conv3(x, w):
    return (pltpu.roll(x, 1, 0) * w[0:1] + x * w[1:2] + pltpu.roll(x, L_PAD - 1, 0) * w[2:3])


def _hyena_order_kernel(conv_in, z_ref, g_ref, wz_ref, wg_ref, skip_ref, f_ref, gm_ref, pq_ref,
                        o_ref, zs_ref, zf_ref, acc_ref):
    fk = pl.program_id(2)

    @pl.when(fk == 0)
    def _():
        z = z_ref[...].astype(F32)
        if conv_in:
            z = _short_conv3(z, wz_ref[...])
        zf_ref[...] = z
        zs_ref[...] = z.astype(BF16)
        acc_ref[...] = jnp.zeros_like(acc_ref)

    zs = zs_ref[...]
    a = jnp.dot(f_ref[0], zs, preferred_element_type=F32)
    b = jnp.dot(f_ref[1], zs, preferred_element_type=F32)
    p = pq_ref[0]
    q = pq_ref[1]
    u = (a * p - b * q).astype(BF16)
    v = (a * q + b * p).astype(BF16)
    acc_ref[...] += (jnp.dot(gm_ref[0], u, preferred_element_type=F32)
                     + jnp.dot(gm_ref[1], v, preferred_element_type=F32))

    @pl.when(fk == pl.num_programs(2) - 1)
    def _():
        gate = _short_conv3(g_ref[...].astype(F32), wg_ref[...])
        o_ref[...] = gate * (acc_ref[...] + skip_ref[...] * zf_ref[...])


def _hyena_order(order, zin, zin_col0, proj, conv_w, skip, fmat, gmat, pq, batch, tc=512):
    conv_in = order == 0
    m = proj.shape[0]
    nct = BRANCH_W // tc
    nfk = N_FREQ // FREQ_CHUNK
    zcb = zin_col0 // tc
    gcb = (COL_B + (order + 1) * BRANCH_W) // tc
    in_specs = [
        pl.BlockSpec((L_PAD, tc), lambda b, c, f: (b, zcb + c)),
        pl.BlockSpec((L_PAD, tc), lambda b, c, f: (b, gcb + c)),
        pl.BlockSpec((3, tc), lambda b, c, f: (0, c)),
        pl.BlockSpec((3, tc), lambda b, c, f: (0, (order + 1) * nct + c)),
        pl.BlockSpec((1, tc), lambda b, c, f: (0, c)),
        pl.BlockSpec((2, FREQ_CHUNK, L_PAD), lambda b, c, f: (0, f, 0)),
        pl.BlockSpec((2, L_PAD, FREQ_CHUNK), lambda b, c, f: (0, 0, f)),
        pl.BlockSpec((2, FREQ_CHUNK, tc), lambda b, c, f: (0, f, order * nct + c)),
    ]
    return pl.pallas_call(
        functools.partial(_hyena_order_kernel, conv_in),
        grid=(batch, nct, nfk),
        in_specs=in_specs,
        out_specs=pl.BlockSpec((L_PAD, tc), lambda b, c, f: (b, c)),
        out_shape=jax.ShapeDtypeStruct((m, BRANCH_W), F32),
        scratch_shapes=[pltpu.VMEM((L_PAD, tc), BF16),
                        pltpu.VMEM((L_PAD, tc), F32),
                        pltpu.VMEM((L_PAD, tc), F32)],
        compiler_params=_cparams(("parallel", "parallel", "arbitrary"), VMEM_CAP),
    )(zin, proj, conv_w, conv_w, skip[order].reshape(1, BRANCH_W), fmat, gmat, pq)


def _pool_kernel(p0_ref, p1_ref, p2_ref, p3_ref, gw_ref, scale_ref, o_ref):
    t = lax.broadcasted_iota(jnp.int32, (L_PAD, 1), 0)
    for g, (w, p_ref) in enumerate(zip(POOL_WINDOWS, (p0_ref, p1_ref, p2_ref, p3_ref))):
        p = p_ref[...].astype(F32)
        s = p
        span = 1
        while span < w:
            s = s + pltpu.roll(s, L_PAD - span, 0)
            span *= 2
        back = (w - 1) // 2
        if back:
            s = pltpu.roll(s, back, 0)
        lo = jnp.clip(t - back, 0, L_TRUE)
        hi = jnp.clip(t + w // 2 + 1, 0, L_TRUE)
        cnt = jnp.maximum(hi - lo, 1).astype(F32)
        d = (s / cnt - p).astype(BF16)
        y = jnp.dot(d, gw_ref[g].astype(BF16), preferred_element_type=F32)
        sl = slice(g * POOL_GROUP_W, (g + 1) * POOL_GROUP_W)
        o_ref[:, sl] = (y * scale_ref[:, sl]).astype(o_ref.dtype)


def _pool(proj, group_w, scale, batch):
    m = proj.shape[0]
    cb = COL_C // POOL_GROUP_W
    grp = lambda g: pl.BlockSpec((L_PAD, POOL_GROUP_W), lambda b: (b, cb + g))
    return pl.pallas_call(
        _pool_kernel,
        grid=(batch,),
        in_specs=[grp(0), grp(1), grp(2), grp(3),
                  pl.BlockSpec((4, POOL_GROUP_W, POOL_GROUP_W), lambda b: (0, 0, 0)),
                  pl.BlockSpec((1, BRANCH_W), lambda b: (0, 0))],
        out_specs=pl.BlockSpec((L_PAD, BRANCH_W), lambda b: (b, 0)),
        out_shape=jax.ShapeDtypeStruct((m, BRANCH_W), BF16),
        compiler_params=_cparams(("parallel",), 48 << 20),
    )(proj, proj, proj, proj, group_w, scale.reshape(1, -1))


def _gate_kernel(tm, oa_ref, zb_ref, yc_ref, od_ref, gate_ref, na_ref, nb_ref, y_ref):
    def rms(x, w):
        ms = jnp.mean(x * x, axis=-1, keepdims=True)
        return x * lax.rsqrt(ms + EPS) * w

    pos = pl.program_id(1) * tm + lax.broadcasted_iota(jnp.int32, (tm, 1), 0)
    live = pos < L_TRUE
    branches = (rms(oa_ref[...].astype(F32), na_ref[...]),
                rms(zb_ref[...], nb_ref[...]),
                yc_ref[...].astype(F32),
                od_ref[...].astype(F32))
    for i, y in enumerate(branches):
        sl = slice(i * BRANCH_W, (i + 1) * BRANCH_W)
        g = gate_ref[:, sl].astype(F32)
        y = y * (g * jax.nn.sigmoid(g))
        y_ref[:, sl] = jnp.where(live, y, 0.0).astype(y_ref.dtype)


def _gate(oa, zb, yc, od, gate, na, nb, batch, tm=272):
    m = oa.shape[0]
    nt = L_PAD // tm
    row = lambda b, t: (b * nt + t, 0)
    full = lambda b, t: (0, 0)
    br = pl.BlockSpec((tm, BRANCH_W), row)
    return pl.pallas_call(
        functools.partial(_gate_kernel, tm),
        grid=(batch, nt),
        in_specs=[br, br, br, br, pl.BlockSpec((tm, D_MODEL), row),
                  pl.BlockSpec((1, BRANCH_W), full), pl.BlockSpec((1, BRANCH_W), full)],
        out_specs=pl.BlockSpec((tm, D_MODEL), row),
        out_shape=jax.ShapeDtypeStruct((m, D_MODEL), BF16),
        compiler_params=_cparams(("parallel", "parallel"), 32 << 20),
    )(oa, zb, yc, od, gate, na.reshape(1, -1), nb.reshape(1, -1))


def _inv_freq(dim, theta):
    return theta ** (-jnp.arange(0, dim, 2, dtype=F32) / dim)


def _pad_rows(t):
    return jnp.pad(t, ((0, L_PAD - t.shape[0]), (0, 0)))


def _rope_tables():
    rows = SEQ_REAL // GRID_W
    zeros_meta = jnp.zeros((N_META,), F32)
    row = jnp.concatenate([zeros_meta, jnp.repeat(jnp.arange(rows, dtype=F32), GRID_W)])
    col = jnp.concatenate([zeros_meta, jnp.tile(jnp.arange(GRID_W, dtype=F32), rows)])
    axf = _inv_freq(HEAD_DIM // 2, AXIAL_THETA)
    ar = row[:, None] * axf[None, :]
    ac = col[:, None] * axf[None, :]
    cos_a = jnp.concatenate([jnp.cos(ar), jnp.cos(ar), jnp.cos(ac), jnp.cos(ac)], axis=-1)
    sin_a = jnp.concatenate([-jnp.sin(ar), jnp.sin(ar), -jnp.sin(ac), jnp.sin(ac)], axis=-1)
    a1 = jnp.arange(L_TRUE, dtype=F32)[:, None] * _inv_freq(ROPE_DIMS, ROPE_THETA)[None, :]
    rest = HEAD_DIM - ROPE_DIMS
    cos_d = jnp.concatenate([jnp.cos(a1), jnp.cos(a1), jnp.ones((L_TRUE, rest), F32)], axis=-1)
    sin_d = jnp.concatenate([-jnp.sin(a1), jnp.sin(a1), jnp.zeros((L_TRUE, rest), F32)], axis=-1)
    return _pad_rows(cos_a), _pad_rows(sin_a), _pad_rows(cos_d), _pad_rows(sin_d)


def _dft_tables():
    k = jnp.arange(N_FREQ, dtype=jnp.int32)[:, None]
    t = jnp.arange(L_PAD, dtype=jnp.int32)[None, :]
    ang = ((k * t) % DFT_N).astype(F32) * (2.0 * math.pi / DFT_N)
    live = t < L_TRUE
    fmat = jnp.stack([jnp.where(live, jnp.cos(ang), 0.0), jnp.where(live, jnp.sin(ang), 0.0)])
    fmat = fmat.astype(BF16)
    return fmat, jnp.swapaxes(fmat, 1, 2)


def _hyena_tables():
    t = jnp.linspace(0.0, 1.0, L_TRUE, dtype=F32)[:, None]
    bands = (HYENA_EMB_DIM - 1) // 2
    fb = jnp.linspace(1e-4, bands - 1, bands, dtype=F32)[None, :]
    wpos = 2.0 * math.pi * jnp.arange(L_TRUE, dtype=F32)[:, None] / L_TRUE
    z = jnp.concatenate([t, jnp.cos(fb * wpos), -jnp.sin(fb * wpos)], axis=-1)
    z = jnp.pad(z, ((0, L_PAD - L_TRUE), (0, LANE - HYENA_EMB_DIM)))
    deltas = jnp.abs(jnp.linspace(math.log(HYENA_TARGET) / HYENA_SLOW,
                                  math.log(HYENA_TARGET) / HYENA_FAST, BRANCH_W, dtype=F32))
    decay = _pad_rows(jnp.exp(-t * deltas) + HYENA_SHIFT)
    return z, decay


def _pad2(a, rows, cols):
    return jnp.pad(a, ((0, rows - a.shape[0]), (0, cols - a.shape[1])))


def kernel(x, meta_tokens, norm_w, w_in, w_out, a_q_norm, a_k_norm, a_out_norm, b_short_conv,
           b_filt_w1, b_filt_b1, b_filt_w2, b_filt_b2, b_filt_w3, b_sin_freq, b_skip, b_out_norm,
           c_group_w, c_scale, d_q_norm, d_k_norm, d_lambda, d_out_norm):
    batch = x.shape[0]
    depth = norm_w.shape[0]
    meta = jnp.broadcast_to(meta_tokens.astype(x.dtype)[None], (batch, N_META, D_MODEL))
    pad = jnp.zeros((batch, L_PAD - L_TRUE, D_MODEL), x.dtype)
    h = jnp.concatenate([meta, x, pad], axis=1).reshape(batch * L_PAD, D_MODEL)

    cos_a, sin_a, cos_d, sin_d = _rope_tables()
    fmat, gmat = _dft_tables()
    zfeat, decay = _hyena_tables()
    bias = jnp.where(jnp.arange(L_PAD) < L_TRUE, 0.0, NEG_BIG).astype(F32).reshape(1, L_PAD)

    for l in range(depth):
        lam_init = 0.8 - 0.6 * math.exp(-0.3 * l)
        u = _rmsnorm(h, norm_w[l])
        proj = _matmul(u, w_in, l, 0, MAIN_W, BF16)
        gate = _matmul(u, w_in, l, COL_GATE, D_MODEL, BF16)

        oa = _attn_a(proj, cos_a, sin_a, a_q_norm[l], a_k_norm[l], bias, batch)
        od = _attn_d(proj, cos_d, sin_d, d_q_norm[l], d_k_norm[l], d_lambda[l], d_out_norm[l],
                     bias, lam_init, batch)

        hsd = _hyena_filter(
            zfeat,
            _pad2(b_filt_w1[l], LANE, LANE), _pad2(b_filt_b1[l][None], 1, LANE),
            _pad2(b_filt_w2[l], LANE, LANE), _pad2(b_filt_b2[l][None], 1, LANE),
            _pad2(b_sin_freq[l][None], 1, LANE), _pad2(b_filt_w3[l], LANE, 4 * BRANCH_W), decay)
        pq = _spectra(fmat, hsd)
        z1 = _hyena_order(0, proj, COL_B, proj, b_short_conv[l], b_skip[l], fmat, gmat, pq, batch)
        z2 = _hyena_order(1, z1, 0, proj, b_short_conv[l], b_skip[l], fmat, gmat, pq, batch)

        yc = _pool(proj, c_group_w[l], c_scale[l], batch)
        y = _gate(oa, z2, yc, od, gate, a_out_norm[l], b_out_norm[l], batch)
        h = _matmul(y, w_out, l, 0, D_MODEL, F32, residual=h)

    return h.reshape(batch, L_PAD, D_MODEL)[:, N_META:L_TRUE]
```

```python
import functools
import math

import jax
import jax.numpy as jnp
from jax import lax
from jax.experimental import pallas as pl
from jax.experimental.pallas import tpu as pltpu

F32 = jnp.float32
BF16 = jnp.bfloat16

D_MODEL = 4096
N_META = 16
GRID_W = 64
EPS = 1e-6
BRANCH_W = 1024
HEAD_DIM = 128
A_HEADS = 8
A_KV_HEADS = 2
A_GROUPS = A_HEADS // A_KV_HEADS
AXIAL_THETA = 10000.0
HYENA_EMB_DIM = 33
HYENA_FFN = 64
HYENA_FAST = 0.3
HYENA_SLOW = 1.5
HYENA_TARGET = 1e-2
HYENA_SHIFT = 0.05
POOL_WINDOWS = (2, 4, 8, 16)
POOL_GROUP_W = 256
D_HEADS = 4
D_HEAD_V = 256
ROPE_THETA = 500000.0
ROPE_DIMS = 32

COL_AQ = 0
COL_AK = 1024
COL_AV = 1280
COL_B = 1536
COL_C = 4608
COL_DQ = 5632
COL_DK = 6656
COL_DV = 7680
COL_GATE = 8704
MAIN_W = COL_GATE
IN_COLS = 12800

LANE = 128
SEQ_REAL = 2048
L_TRUE = N_META + SEQ_REAL
L_PAD = 2176
DFT_N = 4607
N_FREQ = 2304
FREQ_CHUNK = 256
NEG_BIG = -1e30
LOG2E = 1.4426950408889634
Q_SCALE = (HEAD_DIM ** -0.5) * LOG2E
assert L_TRUE > L_PAD - LANE
VMEM_CAP = 56 * 1024 * 1024


def _cparams(semantics, vmem_bytes):
    return pltpu.CompilerParams(dimension_semantics=semantics,
                                vmem_limit_bytes=min(int(vmem_bytes), VMEM_CAP))


def _rmsnorm_kernel(h_ref, w_ref, o_ref):
    x = h_ref[...]
    ms = jnp.mean(x * x, axis=-1, keepdims=True)
    o_ref[...] = (x * lax.rsqrt(ms + EPS) * w_ref[...]).astype(o_ref.dtype)


def _rmsnorm(h, w, tm=272):
    m, d = h.shape
    return pl.pallas_call(
        _rmsnorm_kernel,
        grid=(m // tm,),
        in_specs=[pl.BlockSpec((tm, d), lambda i: (i, 0)),
                  pl.BlockSpec((1, d), lambda i: (0, 0))],
        out_specs=pl.BlockSpec((tm, d), lambda i: (i, 0)),
        out_shape=jax.ShapeDtypeStruct((m, d), BF16),
        compiler_params=_cparams(("parallel",), 32 << 20),
    )(h, w.reshape(1, d))


def _matmul_kernel(x_ref, w_ref, o_ref, wb_ref):
    @pl.when(pl.program_id(1) == 0)
    def _():
        wb_ref[...] = w_ref[...].astype(BF16)

    o_ref[...] = jnp.dot(x_ref[...], wb_ref[...],
                         preferred_element_type=F32).astype(o_ref.dtype)


def _matmul_res_kernel(x_ref, w_ref, r_ref, o_ref, wb_ref):
    @pl.when(pl.program_id(1) == 0)
    def _():
        wb_ref[...] = w_ref[...].astype(BF16)

    o_ref[...] = r_ref[...] + jnp.dot(x_ref[...], wb_ref[...], preferred_element_type=F32)


def _matmul(x, w3, layer, col0, width, out_dtype, residual=None, tm=1088, tn=512):
    m, k = x.shape
    cb0 = col0 // tn
    grid = (width // tn, m // tm)
    in_specs = [pl.BlockSpec((tm, k), lambda j, i: (i, 0)),
                pl.BlockSpec((None, k, tn), lambda j, i: (layer, 0, cb0 + j))]
    args = [x, w3]
    kern = _matmul_kernel
    if residual is not None:
        in_specs.append(pl.BlockSpec((tm, tn), lambda j, i: (i, j)))
        args.append(residual)
        kern = _matmul_res_kernel
    vmem = 2 * (tm * k * 2 + k * tn * 4 + 2 * tm * tn * 4) + k * tn * 2 + (4 << 20)
    return pl.pallas_call(
        kern,
        grid=grid,
        in_specs=in_specs,
        out_specs=pl.BlockSpec((tm, tn), lambda j, i: (i, j)),
        out_shape=jax.ShapeDtypeStruct((m, width), out_dtype),
        scratch_shapes=[pltpu.VMEM((k, tn), BF16)],
        compiler_params=_cparams(("parallel", "arbitrary"), vmem),
    )(*args)


def _norm_rope(x, gain, cos, sin_signed, half_rot):
    ms = jnp.mean(x * x, axis=-1, keepdims=True)
    xn = x * lax.rsqrt(ms + EPS) * gain
    lane = lax.broadcasted_iota(jnp.int32, (1, LANE), 1)
    partner_up = (lane & half_rot) == 0
    partner = jnp.where(partner_up,
                        pltpu.roll(xn, LANE - half_rot, 1),
                        pltpu.roll(xn, half_rot, 1))
    return xn * cos + partner * sin_signed


def _softmax_pv(q_bf, k_bf, v_bf, bias_last, denom_from_ones):
    s = lax.dot_general(q_bf, k_bf, (((1,), (1,)), ((), ())), preferred_element_type=F32)
    n_main = L_PAD - LANE
    s_main = s[:, :n_main]
    s_last = s[:, n_main:] + bias_last
    m = jnp.maximum(jnp.max(s_main, axis=-1, keepdims=True),
                    jnp.max(s_last, axis=-1, keepdims=True))
    p_main = jnp.exp2(s_main - m)
    p_last = jnp.exp2(s_last - m)
    p = jnp.concatenate([p_main.astype(BF16), p_last.astype(BF16)], axis=1)
    o = jnp.dot(p, v_bf, preferred_element_type=F32)
    if denom_from_ones:
        return o[:, :HEAD_DIM] * (1.0 / o[:, HEAD_DIM:HEAD_DIM + 1])
    l = jnp.sum(p_main, axis=-1, keepdims=True) + jnp.sum(p_last, axis=-1, keepdims=True)
    return o * (1.0 / l)


def _attn_a_kernel(sub, q_ref, k_ref, v_ref, cq_ref, sq_ref, ck_ref, sk_ref, gq_ref, gk_ref,
                   bias_ref, o_ref, kh_ref, vx_ref, qh_ref):
    @pl.when((pl.program_id(2) == 0) & (pl.program_id(3) == 0))
    def _():
        kh_ref[...] = _norm_rope(k_ref[...].astype(F32), gk_ref[...], ck_ref[...], sk_ref[...],
                                 HEAD_DIM // 4).astype(BF16)
        vx_ref[:, :HEAD_DIM] = v_ref[...]
        vx_ref[:, HEAD_DIM:] = jnp.ones((L_PAD, HEAD_DIM), BF16)

    q = _norm_rope(q_ref[...].astype(F32), gq_ref[...], cq_ref[...], sq_ref[...], HEAD_DIM // 4)
    qh_ref[...] = (q * Q_SCALE).astype(BF16)
    bias_last = bias_ref[:, L_PAD - LANE:]
    for i in range(q_ref.shape[0] // sub):
        rows = slice(i * sub, (i + 1) * sub)
        o_ref[rows, :] = _softmax_pv(qh_ref[rows, :], kh_ref[...], vx_ref[...], bias_last,
                                     True).astype(o_ref.dtype)


def _attn_a(proj, cos_t, sin_t, gq, gk, bias, batch, tq=1088, sub=272):
    m = proj.shape[0]
    nq = L_PAD // tq
    qblk = lambda b, kv, g, qi: (b * nq + qi, COL_AQ // HEAD_DIM + kv * A_GROUPS + g)
    tabq = lambda b, kv, g, qi: (qi, 0)
    full = lambda b, kv, g, qi: (0, 0)
    return pl.pallas_call(
        functools.partial(_attn_a_kernel, sub),
        grid=(batch, A_KV_HEADS, A_GROUPS, nq),
        in_specs=[
            pl.BlockSpec((tq, HEAD_DIM), qblk),
            pl.BlockSpec((L_PAD, HEAD_DIM), lambda b, kv, g, qi: (b, COL_AK // HEAD_DIM + kv)),
            pl.BlockSpec((L_PAD, HEAD_DIM), lambda b, kv, g, qi: (b, COL_AV // HEAD_DIM + kv)),
            pl.BlockSpec((tq, HEAD_DIM), tabq),
            pl.BlockSpec((tq, HEAD_DIM), tabq),
            pl.BlockSpec((L_PAD, HEAD_DIM), full),
            pl.BlockSpec((L_PAD, HEAD_DIM), full),
            pl.BlockSpec((1, HEAD_DIM), full),
            pl.BlockSpec((1, HEAD_DIM), full),
            pl.BlockSpec((1, L_PAD), full),
        ],
        out_specs=pl.BlockSpec((tq, HEAD_DIM), lambda b, kv, g, qi: (b * nq + qi, kv * A_GROUPS + g)),
        out_shape=jax.ShapeDtypeStruct((m, BRANCH_W), BF16),
        scratch_shapes=[pltpu.VMEM((L_PAD, HEAD_DIM), BF16),
                        pltpu.VMEM((L_PAD, 2 * HEAD_DIM), BF16),
                        pltpu.VMEM((tq, HEAD_DIM), BF16)],
        compiler_params=_cparams(("parallel", "parallel", "arbitrary", "arbitrary"), 48 << 20),
    )(proj, proj, proj, cos_t, sin_t, cos_t, sin_t, gq.reshape(1, -1), gk.reshape(1, -1), bias)


def _attn_d_kernel(lam_init, sub, q_ref, k_ref, v_ref, cq_ref, sq_ref, ck_ref, sk_ref, gq_ref,
                   gk_ref, lam_ref, gout_ref, bias_ref, o_ref, kh_ref, qh_ref):
    halves = (slice(0, HEAD_DIM), slice(HEAD_DIM, 2 * HEAD_DIM))

    @pl.when(pl.program_id(2) == 0)
    def _():
        for sl in halves:
            kh_ref[:, sl] = _norm_rope(k_ref[:, sl].astype(F32), gk_ref[...], ck_ref[...],
                                       sk_ref[...], ROPE_DIMS // 2).astype(BF16)

    for sl in halves:
        q = _norm_rope(q_ref[:, sl].astype(F32), gq_ref[...], cq_ref[...], sq_ref[...],
                       ROPE_DIMS // 2)
        qh_ref[:, sl] = (q * Q_SCALE).astype(BF16)
    lf = lam_ref[...]
    lam = (jnp.exp(jnp.sum(lf[0:1] * lf[1:2], axis=-1, keepdims=True))
           - jnp.exp(jnp.sum(lf[2:3] * lf[3:4], axis=-1, keepdims=True)) + lam_init)
    bias_last = bias_ref[:, L_PAD - LANE:]
    for i in range(q_ref.shape[0] // sub):
        rows = slice(i * sub, (i + 1) * sub)
        o1, o2 = [_softmax_pv(qh_ref[rows, sl], kh_ref[:, sl], v_ref[...], bias_last, False)
                  for sl in halves]
        o = o1 - lam * o2
        ms = jnp.mean(o * o, axis=-1, keepdims=True)
        o = o * lax.rsqrt(ms + EPS) * gout_ref[...] * (1.0 - lam_init)
        o_ref[rows, :] = o.astype(o_ref.dtype)


def _attn_d(proj, cos_t, sin_t, gq, gk, lam_vec, gout, bias, lam_init, batch, tq=1088, sub=272):
    m = proj.shape[0]
    nq = L_PAD // tq
    w = 2 * HEAD_DIM
    tabq = lambda b, h, qi: (qi, 0)
    full = lambda b, h, qi: (0, 0)
    return pl.pallas_call(
        functools.partial(_attn_d_kernel, lam_init, sub),
        grid=(batch, D_HEADS, nq),
        in_specs=[
            pl.BlockSpec((tq, w), lambda b, h, qi: (b * nq + qi, COL_DQ // w + h)),
            pl.BlockSpec((L_PAD, w), lambda b, h, qi: (b, COL_DK // w + h)),
            pl.BlockSpec((L_PAD, w), lambda b, h, qi: (b, COL_DV // w + h)),
            pl.BlockSpec((tq, HEAD_DIM), tabq),
            pl.BlockSpec((tq, HEAD_DIM), tabq),
            pl.BlockSpec((L_PAD, HEAD_DIM), full),
            pl.BlockSpec((L_PAD, HEAD_DIM), full),
            pl.BlockSpec((1, HEAD_DIM), full),
            pl.BlockSpec((1, HEAD_DIM), full),
            pl.BlockSpec((4, HEAD_DIM), full),
            pl.BlockSpec((1, w), full),
            pl.BlockSpec((1, L_PAD), full),
        ],
        out_specs=pl.BlockSpec((tq, w), lambda b, h, qi: (b * nq + qi, h)),
        out_shape=jax.ShapeDtypeStruct((m, BRANCH_W), BF16),
        scratch_shapes=[pltpu.VMEM((L_PAD, w), BF16), pltpu.VMEM((tq, w), BF16)],
        compiler_params=_cparams(("parallel", "parallel", "arbitrary"), 48 << 20),
    )(proj, proj, proj, cos_t, sin_t, cos_t, sin_t, gq.reshape(1, -1), gk.reshape(1, -1),
      lam_vec, gout.reshape(1, -1), bias)


def _hyena_filter_kernel(z_ref, w1_ref, b1_ref, w2_ref, b2_ref, fr_ref, w3f_ref, w3b_ref,
                         dec_ref, o_ref, hid_ref):
    hp = lax.Precision.HIGHEST

    @pl.when((pl.program_id(0) == 0) & (pl.program_id(1) == 0))
    def _():
        fr = fr_ref[...]
        h1 = jnp.sin(fr * (jnp.dot(z_ref[...], w1_ref[...], precision=hp,
                                   preferred_element_type=F32) + b1_ref[...]))
        hid_ref[...] = jnp.sin(fr * (jnp.dot(h1, w2_ref[...], precision=hp,
                                             preferred_element_type=F32) + b2_ref[...]))

    hid = hid_ref[...]
    dec = dec_ref[...]
    lag = lax.broadcasted_iota(jnp.int32, (L_PAD, 1), 0)
    hf = jnp.dot(hid, w3f_ref[...], precision=hp, preferred_element_type=F32) * dec
    hb = jnp.dot(hid, w3b_ref[...], precision=hp, preferred_element_type=F32) * dec
    hf = jnp.where(lag < L_TRUE, hf, 0.0)
    hb = jnp.where((lag >= 1) & (lag < L_TRUE), hb, 0.0)
    o_ref[0] = (hf + hb).astype(o_ref.dtype)
    o_ref[1] = (hf - hb).astype(o_ref.dtype)


def _hyena_filter(zfeat, w1, b1, w2, b2, fr, w3, decay, tc=512):
    nct = BRANCH_W // tc
    full = lambda o, c: (0, 0)
    return pl.pallas_call(
        _hyena_filter_kernel,
        grid=(2, nct),
        in_specs=[
            pl.BlockSpec((L_PAD, LANE), full),
            pl.BlockSpec((LANE, LANE), full),
            pl.BlockSpec((1, LANE), full),
            pl.BlockSpec((LANE, LANE), full),
            pl.BlockSpec((1, LANE), full),
            pl.BlockSpec((1, LANE), full),
            pl.BlockSpec((LANE, tc), lambda o, c: (0, (o * 2) * nct + c)),
            pl.BlockSpec((LANE, tc), lambda o, c: (0, (o * 2 + 1) * nct + c)),
            pl.BlockSpec((L_PAD, tc), lambda o, c: (0, c)),
        ],
        out_specs=pl.BlockSpec((2, L_PAD, tc), lambda o, c: (0, 0, o * nct + c)),
        out_shape=jax.ShapeDtypeStruct((2, L_PAD, 2 * BRANCH_W), BF16),
        scratch_shapes=[pltpu.VMEM((L_PAD, LANE), F32)],
        compiler_params=_cparams(("arbitrary", "arbitrary"), 48 << 20),
    )(zfeat, w1, b1, w2, b2, fr, w3, w3, decay)


def _spectra_kernel(tf, f_ref, h_ref, o_ref):
    k = pl.program_id(1) * tf + lax.broadcasted_iota(jnp.int32, (tf, 1), 0)
    wk = jnp.where(k == 0, 1.0 / DFT_N, 2.0 / DFT_N)
    o_ref[...] = jnp.dot(f_ref[...], h_ref[...], preferred_element_type=F32) * wk


def _spectra(fmat, hsd, tf=768, tn=512):
    ncol = hsd.shape[-1]
    return pl.pallas_call(
        functools.partial(_spectra_kernel, tf),
        grid=(2, N_FREQ // tf, ncol // tn),
        in_specs=[pl.BlockSpec((None, tf, L_PAD), lambda p, f, n: (p, f, 0)),
                  pl.BlockSpec((None, L_PAD, tn), lambda p, f, n: (p, 0, n))],
        out_specs=pl.BlockSpec((None, tf, tn), lambda p, f, n: (p, f, n)),
        out_shape=jax.ShapeDtypeStruct((2, N_FREQ, ncol), F32),
        compiler_params=_cparams(("parallel", "parallel", "arbitrary"), 32 << 20),
    )(fmat, hsd)


def _short_conv3(x, w):
    return (pltpu.roll(x, 1, 0) * w[0:1] + x * w[1:2] + pltpu.roll(x, L_PAD - 1, 0) * w[2:3])


def _hyena_order_kernel(conv_in, z_ref, g_ref, wz_ref, wg_ref, skip_ref, f_ref, gm_ref, pq_ref,
                        o_ref, zs_ref, zf_ref, acc_ref):
    fk = pl.program_id(2)

    @pl.when(fk == 0)
    def _():
        z = z_ref[...].astype(F32)
        if conv_in:
            z = _short_conv3(z, wz_ref[...])
        zf_ref[...] = z
        zs_ref[...] = z.astype(BF16)
        acc_ref[...] = jnp.zeros_like(acc_ref)

    zs = zs_ref[...]
    a = jnp.dot(f_ref[0], zs, preferred_element_type=F32)
    b = jnp.dot(f_ref[1], zs, preferred_element_type=F32)
    p = pq_ref[0]
    q = pq_ref[1]
    u = (a * p - b * q).astype(BF16)
    v = (a * q + b * p).astype(BF16)
    acc_ref[...] += (jnp.dot(gm_ref[0], u, preferred_element_type=F32)
                     + jnp.dot(gm_ref[1], v, preferred_element_type=F32))

    @pl.when(fk == pl.num_programs(2) - 1)
    def _():
        gate = _short_conv3(g_ref[...].astype(F32), wg_ref[...])
        o_ref[...] = gate * (acc_ref[...] + skip_ref[...] * zf_ref[...])


def _hyena_order(order, zin, zin_col0, proj, conv_w, skip, fmat, gmat, pq, batch, tc=512):
    conv_in = order == 0
    m = proj.shape[0]
    nct = BRANCH_W // tc
    nfk = N_FREQ // FREQ_CHUNK
    zcb = zin_col0 // tc
    gcb = (COL_B + (order + 1) * BRANCH_W) // tc
    in_specs = [
        pl.BlockSpec((L_PAD, tc), lambda b, c, f: (b, zcb + c)),
        pl.BlockSpec((L_PAD, tc), lambda b, c, f: (b, gcb + c)),
        pl.BlockSpec((3, tc), lambda b, c, f: (0, c)),
        pl.BlockSpec((3, tc), lambda b, c, f: (0, (order + 1) * nct + c)),
        pl.BlockSpec((1, tc), lambda b, c, f: (0, c)),
        pl.BlockSpec((2, FREQ_CHUNK, L_PAD), lambda b, c, f: (0, f, 0)),
        pl.BlockSpec((2, L_PAD, FREQ_CHUNK), lambda b, c, f: (0, 0, f)),
        pl.BlockSpec((2, FREQ_CHUNK, tc), lambda b, c, f: (0, f, order * nct + c)),
    ]
    return pl.pallas_call(
        functools.partial(_hyena_order_kernel, conv_in),
        grid=(batch, nct, nfk),
        in_specs=in_specs,
        out_specs=pl.BlockSpec((L_PAD, tc), lambda b, c, f: (b, c)),
        out_shape=jax.ShapeDtypeStruct((m, BRANCH_W), F32),
        scratch_shapes=[pltpu.VMEM((L_PAD, tc), BF16),
                        pltpu.VMEM((L_PAD, tc), F32),
                        pltpu.VMEM((L_PAD, tc), F32)],
        compiler_params=_cparams(("parallel", "parallel", "arbitrary"), VMEM_CAP),
    )(zin, proj, conv_w, conv_w, skip[order].reshape(1, BRANCH_W), fmat, gmat, pq)


def _pool_kernel(p0_ref, p1_ref, p2_ref, p3_ref, gw_ref, scale_ref, o_ref):
    t = lax.broadcasted_iota(jnp.int32, (L_PAD, 1), 0)
    for g, (w, p_ref) in enumerate(zip(POOL_WINDOWS, (p0_ref, p1_ref, p2_ref, p3_ref))):
        p = p_ref[...].astype(F32)
        s = p
        span = 1
        while span < w:
            s = s + pltpu.roll(s, L_PAD - span, 0)
            span *= 2
        back = (w - 1) // 2
        if back:
            s = pltpu.roll(s, back, 0)
        lo = jnp.clip(t - back, 0, L_TRUE)
        hi = jnp.clip(t + w // 2 + 1, 0, L_TRUE)
        cnt = jnp.maximum(hi - lo, 1).astype(F32)
        d = (s / cnt - p).astype(BF16)
        y = jnp.dot(d, gw_ref[g].astype(BF16), preferred_element_type=F32)
        sl = slice(g * POOL_GROUP_W, (g + 1) * POOL_GROUP_W)
        o_ref[:, sl] = (y * scale_ref[:, sl]).astype(o_ref.dtype)


def _pool(proj, group_w, scale, batch):
    m = proj.shape[0]
    cb = COL_C // POOL_GROUP_W
    grp = lambda g: pl.BlockSpec((L_PAD, POOL_GROUP_W), lambda b: (b, cb + g))
    return pl.pallas_call(
        _pool_kernel,
        grid=(batch,),
        in_specs=[grp(0), grp(1), grp(2), grp(3),
                  pl.BlockSpec((4, POOL_GROUP_W, POOL_GROUP_W), lambda b: (0, 0, 0)),
                  pl.BlockSpec((1, BRANCH_W), lambda b: (0, 0))],
        out_specs=pl.BlockSpec((L_PAD, BRANCH_W), lambda b: (b, 0)),
        out_shape=jax.ShapeDtypeStruct((m, BRANCH_W), BF16),
        compiler_params=_cparams(("parallel",), 48 << 20),
    )(proj, proj, proj, proj, group_w, scale.reshape(1, -1))


def _gate_kernel(tm, oa_ref, zb_ref, yc_ref, od_ref, gate_ref, na_ref, nb_ref, y_ref):
    def rms(x, w):
        ms = jnp.mean(x * x, axis=-1, keepdims=True)
        return x * lax.rsqrt(ms + EPS) * w

    pos = pl.program_id(1) * tm + lax.broadcasted_iota(jnp.int32, (tm, 1), 0)
    live = pos < L_TRUE
    branches = (rms(oa_ref[...].astype(F32), na_ref[...]),
                rms(zb_ref[...], nb_ref[...]),
                yc_ref[...].astype(F32),
                od_ref[...].astype(F32))
    for i, y in enumerate(branches):
        sl = slice(i * BRANCH_W, (i + 1) * BRANCH_W)
        g = gate_ref[:, sl].astype(F32)
        y = y * (g * jax.nn.sigmoid(g))
        y_ref[:, sl] = jnp.where(live, y, 0.0).astype(y_ref.dtype)


def _gate(oa, zb, yc, od, gate, na, nb, batch, tm=272):
    m = oa.shape[0]
    nt = L_PAD // tm
    row = lambda b, t: (b * nt + t, 0)
    full = lambda b, t: (0, 0)
    br = pl.BlockSpec((tm, BRANCH_W), row)
    return pl.pallas_call(
        functools.partial(_gate_kernel, tm),
        grid=(batch, nt),
        in_specs=[br, br, br, br, pl.BlockSpec((tm, D_MODEL), row),
                  pl.BlockSpec((1, BRANCH_W), full), pl.BlockSpec((1, BRANCH_W), full)],
        out_specs=pl.BlockSpec((tm, D_MODEL), row),
        out_shape=jax.ShapeDtypeStruct((m, D_MODEL), BF16),
        compiler_params=_cparams(("parallel", "parallel"), 32 << 20),
    )(oa, zb, yc, od, gate, na.reshape(1, -1), nb.reshape(1, -1))


def _inv_freq(dim, theta):
    return theta ** (-jnp.arange(0, dim, 2, dtype=F32) / dim)


def _pad_rows(t):
    return jnp.pad(t, ((0, L_PAD - t.shape[0]), (0, 0)))


def _rope_tables():
    rows = SEQ_REAL // GRID_W
    zeros_meta = jnp.zeros((N_META,), F32)
    row = jnp.concatenate([zeros_meta, jnp.repeat(jnp.arange(rows, dtype=F32), GRID_W)])
    col = jnp.concatenate([zeros_meta, jnp.tile(jnp.arange(GRID_W, dtype=F32), rows)])
    axf = _inv_freq(HEAD_DIM // 2, AXIAL_THETA)
    ar = row[:, None] * axf[None, :]
    ac = col[:, None] * axf[None, :]
    cos_a = jnp.concatenate([jnp.cos(ar), jnp.cos(ar), jnp.cos(ac), jnp.cos(ac)], axis=-1)
    sin_a = jnp.concatenate([-jnp.sin(ar), jnp.sin(ar), -jnp.sin(ac), jnp.sin(ac)], axis=-1)
    a1 = jnp.arange(L_TRUE, dtype=F32)[:, None] * _inv_freq(ROPE_DIMS, ROPE_THETA)[None, :]
    rest = HEAD_DIM - ROPE_DIMS
    cos_d = jnp.concatenate([jnp.cos(a1), jnp.cos(a1), jnp.ones((L_TRUE, rest), F32)], axis=-1)
    sin_d = jnp.concatenate([-jnp.sin(a1), jnp.sin(a1), jnp.zeros((L_TRUE, rest), F32)], axis=-1)
    return _pad_rows(cos_a), _pad_rows(sin_a), _pad_rows(cos_d), _pad_rows(sin_d)


def _dft_tables():
    k = jnp.arange(N_FREQ, dtype=jnp.int32)[:, None]
    t = jnp.arange(L_PAD, dtype=jnp.int32)[None, :]
    ang = ((k * t) % DFT_N).astype(F32) * (2.0 * math.pi / DFT_N)
    live = t < L_TRUE
    fmat = jnp.stack([jnp.where(live, jnp.cos(ang), 0.0), jnp.where(live, jnp.sin(ang), 0.0)])
    fmat = fmat.astype(BF16)
    return fmat, jnp.swapaxes(fmat, 1, 2)


def _hyena_tables():
    t = jnp.linspace(0.0, 1.0, L_TRUE, dtype=F32)[:, None]
    bands = (HYENA_EMB_DIM - 1) // 2
    fb = jnp.linspace(1e-4, bands - 1, bands, dtype=F32)[None, :]
    wpos = 2.0 * math.pi * jnp.arange(L_TRUE, dtype=F32)[:, None] / L_TRUE
    z = jnp.concatenate([t, jnp.cos(fb * wpos), -jnp.sin(fb * wpos)], axis=-1)
    z = jnp.pad(z, ((0, L_PAD - L_TRUE), (0, LANE - HYENA_EMB_DIM)))
    deltas = jnp.abs(jnp.linspace(math.log(HYENA_TARGET) / HYENA_SLOW,
                                  math.log(HYENA_TARGET) / HYENA_FAST, BRANCH_W, dtype=F32))
    decay = _pad_rows(jnp.exp(-t * deltas) + HYENA_SHIFT)
    return z, decay


def _pad2(a, rows, cols):
    return jnp.pad(a, ((0, rows - a.shape[0]), (0, cols - a.shape[1])))


def kernel(x, meta_tokens, norm_w, w_in, w_out, a_q_norm, a_k_norm, a_out_norm, b_short_conv,
           b_filt_w1, b_filt_b1, b_filt_w2, b_filt_b2, b_filt_w3, b_sin_freq, b_skip, b_out_norm,
           c_group_w, c_scale, d_q_norm, d_k_norm, d_lambda, d_out_norm):
    batch = x.shape[0]
    depth = norm_w.shape[0]
    meta = jnp.broadcast_to(meta_tokens.astype(x.dtype)[None], (batch, N_META, D_MODEL))
    pad = jnp.zeros((batch, L_PAD - L_TRUE, D_MODEL), x.dtype)
    h = jnp.concatenate([meta, x, pad], axis=1).reshape(batch * L_PAD, D_MODEL)

    cos_a, sin_a, cos_d, sin_d = _rope_tables()
    fmat, gmat = _dft_tables()
    zfeat, decay = _hyena_tables()
    bias = jnp.where(jnp.arange(L_PAD) < L_TRUE, 0.0, NEG_BIG).astype(F32).reshape(1, L_PAD)

    for l in range(depth):
        lam_init = 0.8 - 0.6 * math.exp(-0.3 * l)
        u = _rmsnorm(h, norm_w[l])
        proj = _matmul(u, w_in, l, 0, MAIN_W, BF16)
        gate = _matmul(u, w_in, l, COL_GATE, D_MODEL, BF16)

        oa = _attn_a(proj, cos_a, sin_a, a_q_norm[l], a_k_norm[l], bias, batch)
        od = _attn_d(proj, cos_d, sin_d, d_q_norm[l], d_k_norm[l], d_lambda[l], d_out_norm[l],
                     bias, lam_init, batch)

        hsd = _hyena_filter(
            zfeat,
            _pad2(b_filt_w1[l], LANE, LANE), _pad2(b_filt_b1[l][None], 1, LANE),
            _pad2(b_filt_w2[l], LANE, LANE), _pad2(b_filt_b2[l][None], 1, LANE),
            _pad2(b_sin_freq[l][None], 1, LANE), _pad2(b_filt_w3[l], LANE, 4 * BRANCH_W), decay)
        pq = _spectra(fmat, hsd)
        z1 = _hyena_order(0, proj, COL_B, proj, b_short_conv[l], b_skip[l], fmat, gmat, pq, batch)
        z2 = _hyena_order(1, z1, 0, proj, b_short_conv[l], b_skip[l], fmat, gmat, pq, batch)

        yc = _pool(proj, c_group_w[l], c_scale[l], batch)
        y = _gate(oa, z2, yc, od, gate, a_out_norm[l], b_out_norm[l], batch)
        h = _matmul(y, w_out, l, 0, D_MODEL, F32, residual=h)

    return h.reshape(batch, L_PAD, D_MODEL)[:, N_META:L_TRUE]
```

```python
import functools
import math

import jax
import jax.numpy as jnp
from jax import lax
from jax.experimental import pallas as pl
from jax.experimental.pallas import tpu as pltpu

F32 = jnp.float32
BF16 = jnp.bfloat16

D_MODEL = 4096
N_META = 16
GRID_W = 64
EPS = 1e-6
BRANCH_W = 1024
HEAD_DIM = 128
A_HEADS = 8
A_KV_HEADS = 2
A_GROUPS = A_HEADS // A_KV_HEADS
AXIAL_THETA = 10000.0
HYENA_EMB_DIM = 33
HYENA_FFN = 64
HYENA_FAST = 0.3
HYENA_SLOW = 1.5
HYENA_TARGET = 1e-2
HYENA_SHIFT = 0.05
POOL_WINDOWS = (2, 4, 8, 16)
POOL_GROUP_W = 256
D_HEADS = 4
D_HEAD_V = 256
ROPE_THETA = 500000.0
ROPE_DIMS = 32

COL_AQ = 0
COL_AK = 1024
COL_AV = 1280
COL_B = 1536
COL_C = 4608
COL_DQ = 5632
COL_DK = 6656
COL_DV = 7680
COL_GATE = 8704
MAIN_W = COL_GATE
IN_COLS = 12800

LANE = 128
SEQ_REAL = 2048
L_TRUE = N_META + SEQ_REAL
L_PAD = 2176
DFT_N = 4607
N_FREQ = 2304
FREQ_CHUNK = 256
N_CHUNK = N_FREQ // FREQ_CHUNK
NEG_BIG = -1e30
LOG2E = 1.4426950408889634
Q_SCALE = (HEAD_DIM ** -0.5) * LOG2E
assert L_TRUE > L_PAD - LANE
VMEM_CAP = 56 * 1024 * 1024


def _cparams(semantics, vmem_bytes):
    return pltpu.CompilerParams(dimension_semantics=semantics,
                                vmem_limit_bytes=min(int(vmem_bytes), VMEM_CAP))


def _to_sequence_order(x):
    return jnp.concatenate([x[SEQ_REAL:L_TRUE], x[:SEQ_REAL], x[L_TRUE:]], axis=0)


def _store_from_sequence_order(o_ref, y, cols=slice(None)):
    o_ref[:SEQ_REAL, cols] = y[N_META:L_TRUE].astype(o_ref.dtype)
    o_ref[SEQ_REAL:L_TRUE, cols] = y[:N_META].astype(o_ref.dtype)
    o_ref[L_TRUE:, cols] = y[L_TRUE:].astype(o_ref.dtype)


def _rmsnorm_kernel(h_ref, w_ref, o_ref):
    x = h_ref[...]
    ms = jnp.mean(x * x, axis=-1, keepdims=True)
    o_ref[...] = (x * lax.rsqrt(ms + EPS) * w_ref[...]).astype(o_ref.dtype)


def _embed_rmsnorm_kernel(n_real_tiles, x_ref, tail_ref, w_ref, h_ref, u_ref):
    def emit(x):
        h_ref[...] = x
        ms = jnp.mean(x * x, axis=-1, keepdims=True)
        u_ref[...] = (x * lax.rsqrt(ms + EPS) * w_ref[...]).astype(u_ref.dtype)

    @pl.when(pl.program_id(1) < n_real_tiles)
    def _():
        emit(x_ref[...])

    @pl.when(pl.program_id(1) == n_real_tiles)
    def _():
        emit(tail_ref[...])


def _embed_rmsnorm(x, tail, w, tm=LANE):
    batch, n_real, d = x.shape
    n_real_tiles = n_real // tm
    row = lambda b, i: (b, i, 0)
    return pl.pallas_call(
        functools.partial(_embed_rmsnorm_kernel, n_real_tiles),
        grid=(batch, L_PAD // tm),
        in_specs=[pl.BlockSpec((None, tm, d), lambda b, i: (b, jnp.minimum(i, n_real_tiles - 1), 0)),
                  pl.BlockSpec((tm, d), lambda b, i: (0, 0)),
                  pl.BlockSpec((1, d), lambda b, i: (0, 0))],
        out_specs=[pl.BlockSpec((None, tm, d), row), pl.BlockSpec((None, tm, d), row)],
        out_shape=[jax.ShapeDtypeStruct((batch, L_PAD, d), F32),
                   jax.ShapeDtypeStruct((batch, L_PAD, d), BF16)],
        compiler_params=_cparams(("parallel", "arbitrary"), 32 << 20),
    )(x, tail, w.reshape(1, d))


def _rmsnorm(h, w, tm=272):
    m, d = h.shape
    return pl.pallas_call(
        _rmsnorm_kernel,
        grid=(m // tm,),
        in_specs=[pl.BlockSpec((tm, d), lambda i: (i, 0)),
                  pl.BlockSpec((1, d), lambda i: (0, 0))],
        out_specs=pl.BlockSpec((tm, d), lambda i: (i, 0)),
        out_shape=jax.ShapeDtypeStruct((m, d), BF16),
        compiler_params=_cparams(("parallel",), 32 << 20),
    )(h, w.reshape(1, d))


def _matmul_kernel(x_ref, w_ref, o_ref, wb_ref):
    @pl.when(pl.program_id(1) == 0)
    def _():
        wb_ref[...] = w_ref[...].astype(BF16)

    o_ref[...] = jnp.dot(x_ref[...], wb_ref[...],
                         preferred_element_type=F32).astype(o_ref.dtype)


def _matmul_res_kernel(x_ref, w_ref, r_ref, o_ref, wb_ref):
    @pl.when(pl.program_id(1) == 0)
    def _():
        wb_ref[...] = w_ref[...].astype(BF16)

    o_ref[...] = r_ref[...] + jnp.dot(x_ref[...], wb_ref[...], preferred_element_type=F32)


def _matmul(x, w3, layer, col0, width, out_dtype, residual=None, tm=1088, tn=512):
    m, k = x.shape
    cb0 = col0 // tn
    grid = (width // tn, m // tm)
    in_specs = [pl.BlockSpec((tm, k), lambda j, i: (i, 0)),
                pl.BlockSpec((None, k, tn), lambda j, i: (layer, 0, cb0 + j))]
    args = [x, w3]
    kern = _matmul_kernel
    if residual is not None:
        in_specs.append(pl.BlockSpec((tm, tn), lambda j, i: (i, j)))
        args.append(residual)
        kern = _matmul_res_kernel
    vmem = 2 * (tm * k * 2 + k * tn * 4 + 2 * tm * tn * 4) + k * tn * 2 + (4 << 20)
    return pl.pallas_call(
        kern,
        grid=grid,
        in_specs=in_specs,
        out_specs=pl.BlockSpec((tm, tn), lambda j, i: (i, j)),
        out_shape=jax.ShapeDtypeStruct((m, width), out_dtype),
        scratch_shapes=[pltpu.VMEM((k, tn), BF16)],
        compiler_params=_cparams(("parallel", "arbitrary"), vmem),
    )(*args)


def _matmul_final_kernel(x_ref, w_ref, r_ref, o_ref, wb_ref):
    @pl.when((pl.program_id(1) == 0) & (pl.program_id(2) == 0))
    def _():
        wb_ref[...] = w_ref[...].astype(BF16)

    o_ref[...] = r_ref[...] + jnp.dot(x_ref[...], wb_ref[...], preferred_element_type=F32)


def _matmul_final(y3, w3, layer, h3, tm=1024, tn=512):
    batch, _, k = y3.shape
    d = h3.shape[-1]
    vmem = 2 * (tm * k * 2 + k * tn * 4 + 2 * tm * tn * 4) + k * tn * 2 + (4 << 20)
    return pl.pallas_call(
        _matmul_final_kernel,
        grid=(d // tn, batch, SEQ_REAL // tm),
        in_specs=[pl.BlockSpec((None, tm, k), lambda j, b, i: (b, i, 0)),
                  pl.BlockSpec((None, k, tn), lambda j, b, i: (layer, 0, j)),
                  pl.BlockSpec((None, tm, tn), lambda j, b, i: (b, i, j))],
        out_specs=pl.BlockSpec((None, tm, tn), lambda j, b, i: (b, i, j)),
        out_shape=jax.ShapeDtypeStruct((batch, SEQ_REAL, d), F32),
        scratch_shapes=[pltpu.VMEM((k, tn), BF16)],
        compiler_params=_cparams(("parallel", "arbitrary", "arbitrary"), vmem),
    )(y3, w3, h3)


def _norm_rope(x, gain, cos, sin_signed, half_rot):
    ms = jnp.mean(x * x, axis=-1, keepdims=True)
    xn = x * lax.rsqrt(ms + EPS) * gain
    lane = lax.broadcasted_iota(jnp.int32, (1, LANE), 1)
    partner_up = (lane & half_rot) == 0
    partner = jnp.where(partner_up,
                        pltpu.roll(xn, LANE - half_rot, 1),
                        pltpu.roll(xn, half_rot, 1))
    return xn * cos + partner * sin_signed


def _softmax_pv(q_bf, k_bf, v_bf, bias_last, denom_from_ones):
    s = lax.dot_general(q_bf, k_bf, (((1,), (1,)), ((), ())), preferred_element_type=F32)
    n_main = L_PAD - LANE
    s_main = s[:, :n_main]
    s_last = s[:, n_main:] + bias_last
    m = jnp.maximum(jnp.max(s_main, axis=-1, keepdims=True),
                    jnp.max(s_last, axis=-1, keepdims=True))
    p_main = jnp.exp2(s_main - m)
    p_last = jnp.exp2(s_last - m)
    p = jnp.concatenate([p_main.astype(BF16), p_last.astype(BF16)], axis=1)
    o = jnp.dot(p, v_bf, preferred_element_type=F32)
    if denom_from_ones:
        return o[:, :HEAD_DIM] * (1.0 / o[:, HEAD_DIM:HEAD_DIM + 1])
    l = jnp.sum(p_main, axis=-1, keepdims=True) + jnp.sum(p_last, axis=-1, keepdims=True)
    return o * (1.0 / l)


def _attn_a_kernel(sub, q_ref, k_ref, v_ref, cq_ref, sq_ref, ck_ref, sk_ref, gq_ref, gk_ref,
                   bias_ref, o_ref, kh_ref, vx_ref, qh_ref):
    @pl.when((pl.program_id(2) == 0) & (pl.program_id(3) == 0))
    def _():
        kh_ref[...] = _norm_rope(k_ref[...].astype(F32), gk_ref[...], ck_ref[...], sk_ref[...],
                                 HEAD_DIM // 4).astype(BF16)
        vx_ref[:, :HEAD_DIM] = v_ref[...]
        vx_ref[:, HEAD_DIM:] = jnp.ones((L_PAD, HEAD_DIM), BF16)

    q = _norm_rope(q_ref[...].astype(F32), gq_ref[...], cq_ref[...], sq_ref[...], HEAD_DIM // 4)
    qh_ref[...] = (q * Q_SCALE).astype(BF16)
    bias_last = bias_ref[:, L_PAD - LANE:]
    for i in range(q_ref.shape[0] // sub):
        rows = slice(i * sub, (i + 1) * sub)
        o_ref[rows, :] = _softmax_pv(qh_ref[rows, :], kh_ref[...], vx_ref[...], bias_last,
                                     True).astype(o_ref.dtype)


def _attn_a(proj, cos_t, sin_t, gq, gk, bias, batch, tq=1088, sub=272):
    m = proj.shape[0]
    nq = L_PAD // tq
    qblk = lambda b, kv, g, qi: (b * nq + qi, COL_AQ // HEAD_DIM + kv * A_GROUPS + g)
    tabq = lambda b, kv, g, qi: (qi, 0)
    full = lambda b, kv, g, qi: (0, 0)
    return pl.pallas_call(
        functools.partial(_attn_a_kernel, sub),
        grid=(batch, A_KV_HEADS, A_GROUPS, nq),
        in_specs=[
            pl.BlockSpec((tq, HEAD_DIM), qblk),
            pl.BlockSpec((L_PAD, HEAD_DIM), lambda b, kv, g, qi: (b, COL_AK // HEAD_DIM + kv)),
            pl.BlockSpec((L_PAD, HEAD_DIM), lambda b, kv, g, qi: (b, COL_AV // HEAD_DIM + kv)),
            pl.BlockSpec((tq, HEAD_DIM), tabq),
            pl.BlockSpec((tq, HEAD_DIM), tabq),
            pl.BlockSpec((L_PAD, HEAD_DIM), full),
            pl.BlockSpec((L_PAD, HEAD_DIM), full),
            pl.BlockSpec((1, HEAD_DIM), full),
            pl.BlockSpec((1, HEAD_DIM), full),
            pl.BlockSpec((1, L_PAD), full),
        ],
        out_specs=pl.BlockSpec((tq, HEAD_DIM), lambda b, kv, g, qi: (b * nq + qi, kv * A_GROUPS + g)),
        out_shape=jax.ShapeDtypeStruct((m, BRANCH_W), BF16),
        scratch_shapes=[pltpu.VMEM((L_PAD, HEAD_DIM), BF16),
                        pltpu.VMEM((L_PAD, 2 * HEAD_DIM), BF16),
                        pltpu.VMEM((tq, HEAD_DIM), BF16)],
        compiler_params=_cparams(("parallel", "parallel", "arbitrary", "arbitrary"), 48 << 20),
    )(proj, proj, proj, cos_t, sin_t, cos_t, sin_t, gq.reshape(1, -1), gk.reshape(1, -1), bias)


def _attn_d_kernel(lam_init, sub, q_ref, k_ref, v_ref, cq_ref, sq_ref, ck_ref, sk_ref, gq_ref,
                   gk_ref, lam_ref, gout_ref, bias_ref, o_ref, kh_ref, qh_ref):
    halves = (slice(0, HEAD_DIM), slice(HEAD_DIM, 2 * HEAD_DIM))

    @pl.when(pl.program_id(2) == 0)
    def _():
        for sl in halves:
            kh_ref[:, sl] = _norm_rope(k_ref[:, sl].astype(F32), gk_ref[...], ck_ref[...],
                                       sk_ref[...], ROPE_DIMS // 2).astype(BF16)

    for sl in halves:
        q = _norm_rope(q_ref[:, sl].astype(F32), gq_ref[...], cq_ref[...], sq_ref[...],
                       ROPE_DIMS // 2)
        qh_ref[:, sl] = (q * Q_SCALE).astype(BF16)
    lf = lam_ref[...]
    lam = (jnp.exp(jnp.sum(lf[0:1] * lf[1:2], axis=-1, keepdims=True))
           - jnp.exp(jnp.sum(lf[2:3] * lf[3:4], axis=-1, keepdims=True)) + lam_init)
    bias_last = bias_ref[:, L_PAD - LANE:]
    for i in range(q_ref.shape[0] // sub):
        rows = slice(i * sub, (i + 1) * sub)
        o1, o2 = [_softmax_pv(qh_ref[rows, sl], kh_ref[:, sl], v_ref[...], bias_last, False)
                  for sl in halves]
        o = o1 - lam * o2
        ms = jnp.mean(o * o, axis=-1, keepdims=True)
        o = o * lax.rsqrt(ms + EPS) * gout_ref[...] * (1.0 - lam_init)
        o_ref[rows, :] = o.astype(o_ref.dtype)


def _attn_d(proj, cos_t, sin_t, gq, gk, lam_vec, gout, bias, lam_init, batch, tq=1088, sub=272):
    m = proj.shape[0]
    nq = L_PAD // tq
    w = 2 * HEAD_DIM
    tabq = lambda b, h, qi: (qi, 0)
    full = lambda b, h, qi: (0, 0)
    return pl.pallas_call(
        functools.partial(_attn_d_kernel, lam_init, sub),
        grid=(batch, D_HEADS, nq),
        in_specs=[
            pl.BlockSpec((tq, w), lambda b, h, qi: (b * nq + qi, COL_DQ // w + h)),
            pl.BlockSpec((L_PAD, w), lambda b, h, qi: (b, COL_DK // w + h)),
            pl.BlockSpec((L_PAD, w), lambda b, h, qi: (b, COL_DV // w + h)),
            pl.BlockSpec((tq, HEAD_DIM), tabq),
            pl.BlockSpec((tq, HEAD_DIM), tabq),
            pl.BlockSpec((L_PAD, HEAD_DIM), full),
            pl.BlockSpec((L_PAD, HEAD_DIM), full),
            pl.BlockSpec((1, HEAD_DIM), full),
            pl.BlockSpec((1, HEAD_DIM), full),
            pl.BlockSpec((4, HEAD_DIM), full),
            pl.BlockSpec((1, w), full),
            pl.BlockSpec((1, L_PAD), full),
        ],
        out_specs=pl.BlockSpec((tq, w), lambda b, h, qi: (b * nq + qi, h)),
        out_shape=jax.ShapeDtypeStruct((m, BRANCH_W), BF16),
        scratch_shapes=[pltpu.VMEM((L_PAD, w), BF16), pltpu.VMEM((tq, w), BF16)],
        compiler_params=_cparams(("parallel", "parallel", "arbitrary"), 48 << 20),
    )(proj, proj, proj, cos_t, sin_t, cos_t, sin_t, gq.reshape(1, -1), gk.reshape(1, -1),
      lam_vec, gout.reshape(1, -1), bias)


def _hyena_filter_kernel(z_ref, w1_ref, b1_ref, w2_ref, b2_ref, fr_ref, w3f_ref, w3b_ref,
                         dec_ref, o_ref, hid_ref):
    hp = lax.Precision.HIGHEST

    @pl.when((pl.program_id(0) == 0) & (pl.program_id(1) == 0))
    def _():
        fr = fr_ref[...]
        h1 = jnp.sin(fr * (jnp.dot(z_ref[...], w1_ref[...], precision=hp,
                                   preferred_element_type=F32) + b1_ref[...]))
        hid_ref[...] = jnp.sin(fr * (jnp.dot(h1, w2_ref[...], precision=hp,
                                             preferred_element_type=F32) + b2_ref[...]))

    hid = hid_ref[...]
    dec = dec_ref[...]
    lag = lax.broadcasted_iota(jnp.int32, (L_PAD, 1), 0)
    hf = jnp.dot(hid, w3f_ref[...], precision=hp, preferred_element_type=F32) * dec
    hb = jnp.dot(hid, w3b_ref[...], precision=hp, preferred_element_type=F32) * dec
    hf = jnp.where(lag < L_TRUE, hf, 0.0)
    hb = jnp.where((lag >= 1) & (lag < L_TRUE), hb, 0.0)
    o_ref[0] = (hf + hb).astype(o_ref.dtype)
    o_ref[1] = (hf - hb).astype(o_ref.dtype)


def _hyena_filter(zfeat, w1, b1, w2, b2, fr, w3, decay, tc=512):
    nct = BRANCH_W // tc
    full = lambda o, c: (0, 0)
    return pl.pallas_call(
        _hyena_filter_kernel,
        grid=(2, nct),
        in_specs=[
            pl.BlockSpec((L_PAD, LANE), full),
            pl.BlockSpec((LANE, LANE), full),
            pl.BlockSpec((1, LANE), full),
            pl.BlockSpec((LANE, LANE), full),
            pl.BlockSpec((1, LANE), full),
            pl.BlockSpec((1, LANE), full),
            pl.BlockSpec((LANE, tc), lambda o, c: (0, (o * 2) * nct + c)),
            pl.BlockSpec((LANE, tc), lambda o, c: (0, (o * 2 + 1) * nct + c)),
            pl.BlockSpec((L_PAD, tc), lambda o, c: (0, c)),
        ],
        out_specs=pl.BlockSpec((2, L_PAD, tc), lambda o, c: (0, 0, o * nct + c)),
        out_shape=jax.ShapeDtypeStruct((2, L_PAD, 2 * BRANCH_W), BF16),
        scratch_shapes=[pltpu.VMEM((L_PAD, LANE), F32)],
        compiler_params=_cparams(("arbitrary", "arbitrary"), 48 << 20),
    )(zfeat, w1, b1, w2, b2, fr, w3, w3, decay)


def _dft_tables_kernel(bc_ref, bs_ref, btc_ref, bts_ref, sc_ref, ss_ref, sct_ref, sst_ref,
                       f_ref, g_ref):
    sc, ss = sc_ref[...], ss_ref[...]
    bc, bs = bc_ref[...], bs_ref[...]
    f_ref[:FREQ_CHUNK, :] = (sc * bc - ss * bs).astype(f_ref.dtype)
    f_ref[FREQ_CHUNK:, :] = (ss * bc + sc * bs).astype(f_ref.dtype)
    sct, sst = sct_ref[...], sst_ref[...]
    btc, bts = btc_ref[...], bts_ref[...]
    g_ref[:, :FREQ_CHUNK] = (sct * btc - sst * bts).astype(g_ref.dtype)
    g_ref[:, FREQ_CHUNK:] = (sst * btc + sct * bts).astype(g_ref.dtype)


def _dft_tables():
    def trig(k, t):
        ang = ((k * t) % DFT_N).astype(F32) * (2.0 * math.pi / DFT_N)
        return jnp.cos(ang), jnp.sin(ang)

    t = jnp.arange(L_PAD, dtype=jnp.int32)
    bc, bs = trig(jnp.arange(FREQ_CHUNK, dtype=jnp.int32)[:, None], t[None, :])
    live = (t < L_TRUE).astype(F32)[None, :]
    sc, ss = trig((jnp.arange(N_CHUNK, dtype=jnp.int32) * FREQ_CHUNK)[:, None], t[None, :])
    sc, ss = sc * live, ss * live
    full2 = lambda j: (0, 0)
    step_row = pl.BlockSpec((None, 1, L_PAD), lambda j: (j, 0, 0))
    step_col = pl.BlockSpec((None, L_PAD, 1), lambda j: (j, 0, 0))
    return pl.pallas_call(
        _dft_tables_kernel,
        grid=(N_CHUNK,),
        in_specs=[pl.BlockSpec((FREQ_CHUNK, L_PAD), full2), pl.BlockSpec((FREQ_CHUNK, L_PAD), full2),
                  pl.BlockSpec((L_PAD, FREQ_CHUNK), full2), pl.BlockSpec((L_PAD, FREQ_CHUNK), full2),
                  step_row, step_row, step_col, step_col],
        out_specs=[pl.BlockSpec((None, 2 * FREQ_CHUNK, L_PAD), lambda j: (j, 0, 0)),
                   pl.BlockSpec((None, L_PAD, 2 * FREQ_CHUNK), lambda j: (j, 0, 0))],
        out_shape=[jax.ShapeDtypeStruct((N_CHUNK, 2 * FREQ_CHUNK, L_PAD), BF16),
                   jax.ShapeDtypeStruct((N_CHUNK, L_PAD, 2 * FREQ_CHUNK), BF16)],
        compiler_params=_cparams(("parallel",), 48 << 20),
    )(bc, bs, bc.T, bs.T, sc[:, None, :], ss[:, None, :], sc[:, :, None], ss[:, :, None])


def _spectra_kernel(f_ref, h_ref, o_ref):
    k = pl.program_id(0) * FREQ_CHUNK + lax.broadcasted_iota(jnp.int32, (FREQ_CHUNK, 1), 0)
    wk = jnp.where(k == 0, 1.0 / DFT_N, 2.0 / DFT_N)
    o_ref[:FREQ_CHUNK, :] = jnp.dot(f_ref[:FREQ_CHUNK, :], h_ref[0],
                                    preferred_element_type=F32) * wk
    o_ref[FREQ_CHUNK:, :] = jnp.dot(f_ref[FREQ_CHUNK:, :], h_ref[1],
                                    preferred_element_type=F32) * wk


def _spectra(fmat, hsd, tn=512):
    ncol = hsd.shape[-1]
    return pl.pallas_call(
        _spectra_kernel,
        grid=(N_CHUNK, ncol // tn),
        in_specs=[pl.BlockSpec((None, 2 * FREQ_CHUNK, L_PAD), lambda f, n: (f, 0, 0)),
                  pl.BlockSpec((2, L_PAD, tn), lambda f, n: (0, 0, n))],
        out_specs=pl.BlockSpec((None, 2 * FREQ_CHUNK, tn), lambda f, n: (f, 0, n)),
        out_shape=jax.ShapeDtypeStruct((N_CHUNK, 2 * FREQ_CHUNK, ncol), F32),
        compiler_params=_cparams(("parallel", "arbitrary"), 32 << 20),
    )(fmat, hsd)


def _short_conv3(x, w):
    return (pltpu.roll(x, 1, 0) * w[0:1] + x * w[1:2] + pltpu.roll(x, L_PAD - 1, 0) * w[2:3])


def _hyena_order_kernel(first, z_ref, g_ref, wz_ref, wg_ref, skip_ref, f_ref, gm_ref, pq_ref,
                        o_ref, zs_ref, zf_ref, acc_ref):
    fk = pl.program_id(2)

    @pl.when(fk == 0)
    def _():
        z = z_ref[...].astype(F32)
        if first:
            z = _short_conv3(_to_sequence_order(z), wz_ref[...])
        zf_ref[...] = z
        zs_ref[...] = z.astype(BF16)
        acc_ref[...] = jnp.zeros_like(acc_ref)

    ab = jnp.dot(f_ref[...], zs_ref[...], preferred_element_type=F32)
    a, b = ab[:FREQ_CHUNK], ab[FREQ_CHUNK:]
    p, q = pq_ref[:FREQ_CHUNK, :], pq_ref[FREQ_CHUNK:, :]
    uv = jnp.concatenate([(a * p - b * q).astype(BF16), (a * q + b * p).astype(BF16)], axis=0)
    acc_ref[...] += jnp.dot(gm_ref[...], uv, preferred_element_type=F32)

    @pl.when(fk == pl.num_programs(2) - 1)
    def _():
        gate = _short_conv3(_to_sequence_order(g_ref[...].astype(F32)), wg_ref[...])
        y = gate * (acc_ref[...] + skip_ref[...] * zf_ref[...])
        if first:
            o_ref[...] = y
        else:
            _store_from_sequence_order(o_ref, y)


def _hyena_order(order, zin, zin_col0, proj, conv_w, skip, fmat, gmat, pq, batch, tc=512):
    first = order == 0
    m = proj.shape[0]
    nct = BRANCH_W // tc
    zcb = zin_col0 // tc
    gcb = (COL_B + (order + 1) * BRANCH_W) // tc
    in_specs = [
        pl.BlockSpec((L_PAD, tc), lambda b, c, f: (b, zcb + c)),
        pl.BlockSpec((L_PAD, tc), lambda b, c, f: (b, gcb + c)),
        pl.BlockSpec((3, tc), lambda b, c, f: (0, c)),
        pl.BlockSpec((3, tc), lambda b, c, f: (0, (order + 1) * nct + c)),
        pl.BlockSpec((1, tc), lambda b, c, f: (0, c)),
        pl.BlockSpec((None, 2 * FREQ_CHUNK, L_PAD), lambda b, c, f: (f, 0, 0)),
        pl.BlockSpec((None, L_PAD, 2 * FREQ_CHUNK), lambda b, c, f: (f, 0, 0)),
        pl.BlockSpec((None, 2 * FREQ_CHUNK, tc), lambda b, c, f: (f, 0, order * nct + c)),
    ]
    return pl.pallas_call(
        functools.partial(_hyena_order_kernel, first),
        grid=(batch, nct, N_CHUNK),
        in_specs=in_specs,
        out_specs=pl.BlockSpec((L_PAD, tc), lambda b, c, f: (b, c)),
        out_shape=jax.ShapeDtypeStruct((m, BRANCH_W), F32),
        scratch_shapes=[pltpu.VMEM((L_PAD, tc), BF16),
                        pltpu.VMEM((L_PAD, tc), F32),
                        pltpu.VMEM((L_PAD, tc), F32)],
        compiler_params=_cparams(("parallel", "parallel", "arbitrary"), VMEM_CAP),
    )(zin, proj, conv_w, conv_w, skip[order].reshape(1, BRANCH_W), fmat, gmat, pq)


def _pool_kernel(p0_ref, p1_ref, p2_ref, p3_ref, gw_ref, scale_ref, o_ref):
    t = lax.broadcasted_iota(jnp.int32, (L_PAD, 1), 0)
    for g, (w, p_ref) in enumerate(zip(POOL_WINDOWS, (p0_ref, p1_ref, p2_ref, p3_ref))):
        p = _to_sequence_order(p_ref[...].astype(F32))
        s = p
        span = 1
        while span < w:
            s = s + pltpu.roll(s, L_PAD - span, 0)
            span *= 2
        back = (w - 1) // 2
        if back:
            s = pltpu.roll(s, back, 0)
        lo = jnp.clip(t - back, 0, L_TRUE)
        hi = jnp.clip(t + w // 2 + 1, 0, L_TRUE)
        cnt = jnp.maximum(hi - lo, 1).astype(F32)
        d = (s / cnt - p).astype(BF16)
        y = jnp.dot(d, gw_ref[g].astype(BF16), preferred_element_type=F32)
        sl = slice(g * POOL_GROUP_W, (g + 1) * POOL_GROUP_W)
        _store_from_sequence_order(o_ref, y * scale_ref[:, sl], sl)


def _pool(proj, group_w, scale, batch):
    m = proj.shape[0]
    cb = COL_C // POOL_GROUP_W
    grp = lambda g: pl.BlockSpec((L_PAD, POOL_GROUP_W), lambda b: (b, cb + g))
    return pl.pallas_call(
        _pool_kernel,
        grid=(batch,),
        in_specs=[grp(0), grp(1), grp(2), grp(3),
                  pl.BlockSpec((4, POOL_GROUP_W, POOL_GROUP_W), lambda b: (0, 0, 0)),
                  pl.BlockSpec((1, BRANCH_W), lambda b: (0, 0))],
        out_specs=pl.BlockSpec((L_PAD, BRANCH_W), lambda b: (b, 0)),
        out_shape=jax.ShapeDtypeStruct((m, BRANCH_W), BF16),
        compiler_params=_cparams(("parallel",), 48 << 20),
    )(proj, proj, proj, proj, group_w, scale.reshape(1, -1))


def _gate_kernel(tm, oa_ref, zb_ref, yc_ref, od_ref, gate_ref, na_ref, nb_ref, y_ref):
    def rms(x, w):
        ms = jnp.mean(x * x, axis=-1, keepdims=True)
        return x * lax.rsqrt(ms + EPS) * w

    pos = pl.program_id(1) * tm + lax.broadcasted_iota(jnp.int32, (tm, 1), 0)
    live = pos < L_TRUE
    branches = (rms(oa_ref[...].astype(F32), na_ref[...]),
                rms(zb_ref[...], nb_ref[...]),
                yc_ref[...].astype(F32),
                od_ref[...].astype(F32))
    for i, y in enumerate(branches):
        sl = slice(i * BRANCH_W, (i + 1) * BRANCH_W)
        g = gate_ref[:, sl].astype(F32)
        y = y * (g * jax.nn.sigmoid(g))
        y_ref[:, sl] = jnp.where(live, y, 0.0).astype(y_ref.dtype)


def _gate(oa, zb, yc, od, gate, na, nb, batch, tm=272):
    m = oa.shape[0]
    nt = L_PAD // tm
    row = lambda b, t: (b * nt + t, 0)
    full = lambda b, t: (0, 0)
    br = pl.BlockSpec((tm, BRANCH_W), row)
    return pl.pallas_call(
        functools.partial(_gate_kernel, tm),
        grid=(batch, nt),
        in_specs=[br, br, br, br, pl.BlockSpec((tm, D_MODEL), row),
                  pl.BlockSpec((1, BRANCH_W), full), pl.BlockSpec((1, BRANCH_W), full)],
        out_specs=pl.BlockSpec((tm, D_MODEL), row),
        out_shape=jax.ShapeDtypeStruct((m, D_MODEL), BF16),
        compiler_params=_cparams(("parallel", "parallel"), 32 << 20),
    )(oa, zb, yc, od, gate, na.reshape(1, -1), nb.reshape(1, -1))


def _inv_freq(dim, theta):
    return theta ** (-jnp.arange(0, dim, 2, dtype=F32) / dim)


def _pad_rows(t):
    return jnp.pad(t, ((0, L_PAD - t.shape[0]), (0, 0)))


def _rope_tables():
    rows = SEQ_REAL // GRID_W
    zeros_meta = jnp.zeros((N_META,), F32)
    row = jnp.concatenate([zeros_meta, jnp.repeat(jnp.arange(rows, dtype=F32), GRID_W)])
    col = jnp.concatenate([zeros_meta, jnp.tile(jnp.arange(GRID_W, dtype=F32), rows)])
    axf = _inv_freq(HEAD_DIM // 2, AXIAL_THETA)
    ar = row[:, None] * axf[None, :]
    ac = col[:, None] * axf[None, :]
    cos_a = jnp.concatenate([jnp.cos(ar), jnp.cos(ar), jnp.cos(ac), jnp.cos(ac)], axis=-1)
    sin_a = jnp.concatenate([-jnp.sin(ar), jnp.sin(ar), -jnp.sin(ac), jnp.sin(ac)], axis=-1)
    a1 = jnp.arange(L_TRUE, dtype=F32)[:, None] * _inv_freq(ROPE_DIMS, ROPE_THETA)[None, :]
    rest = HEAD_DIM - ROPE_DIMS
    cos_d = jnp.concatenate([jnp.cos(a1), jnp.cos(a1), jnp.ones((L_TRUE, rest), F32)], axis=-1)
    sin_d = jnp.concatenate([-jnp.sin(a1), jnp.sin(a1), jnp.zeros((L_TRUE, rest), F32)], axis=-1)
    def hbm_rows(tab):
        return _pad_rows(jnp.concatenate([tab[N_META:], tab[:N_META]], axis=0))

    return hbm_rows(cos_a), hbm_rows(sin_a), hbm_rows(cos_d), hbm_rows(sin_d)


def _hyena_tables():
    t = jnp.linspace(0.0, 1.0, L_TRUE, dtype=F32)[:, None]
    bands = (HYENA_EMB_DIM - 1) // 2
    fb = jnp.linspace(1e-4, bands - 1, bands, dtype=F32)[None, :]
    wpos = 2.0 * math.pi * jnp.arange(L_TRUE, dtype=F32)[:, None] / L_TRUE
    z = jnp.concatenate([t, jnp.cos(fb * wpos), -jnp.sin(fb * wpos)], axis=-1)
    z = jnp.pad(z, ((0, L_PAD - L_TRUE), (0, LANE - HYENA_EMB_DIM)))
    deltas = jnp.abs(jnp.linspace(math.log(HYENA_TARGET) / HYENA_SLOW,
                                  math.log(HYENA_TARGET) / HYENA_FAST, BRANCH_W, dtype=F32))
    decay = _pad_rows(jnp.exp(-t * deltas) + HYENA_SHIFT)
    return z, decay


def _pad2(a, rows, cols):
    return jnp.pad(a, ((0, rows - a.shape[0]), (0, cols - a.shape[1])))


def kernel(x, meta_tokens, norm_w, w_in, w_out, a_q_norm, a_k_norm, a_out_norm, b_short_conv,
           b_filt_w1, b_filt_b1, b_filt_w2, b_filt_b2, b_filt_w3, b_sin_freq, b_skip, b_out_norm,
           c_group_w, c_scale, d_q_norm, d_k_norm, d_lambda, d_out_norm):
    batch = x.shape[0]
    depth = norm_w.shape[0]
    m = batch * L_PAD
    tail = _pad2(meta_tokens.astype(x.dtype), L_PAD - SEQ_REAL, D_MODEL)

    cos_a, sin_a, cos_d, sin_d = _rope_tables()
    fmat, gmat = _dft_tables()
    zfeat, decay = _hyena_tables()
    bias = jnp.where(jnp.arange(L_PAD) < L_TRUE, 0.0, NEG_BIG).astype(F32).reshape(1, L_PAD)

    for l in range(depth):
        lam_init = 0.8 - 0.6 * math.exp(-0.3 * l)
        if l == 0:
            h, u = _embed_rmsnorm(x, tail, norm_w[0])
            h, u = h.reshape(m, D_MODEL), u.reshape(m, D_MODEL)
        else:
            u = _rmsnorm(h, norm_w[l])
        proj = _matmul(u, w_in, l, 0, MAIN_W, BF16)
        gate = _matmul(u, w_in, l, COL_GATE, D_MODEL, BF16)

        oa = _attn_a(proj, cos_a, sin_a, a_q_norm[l], a_k_norm[l], bias, batch)
        od = _attn_d(proj, cos_d, sin_d, d_q_norm[l], d_k_norm[l], d_lambda[l], d_out_norm[l],
                     bias, lam_init, batch)

        hsd = _hyena_filter(
            zfeat,
            _pad2(b_filt_w1[l], LANE, LANE), _pad2(b_filt_b1[l][None], 1, LANE),
            _pad2(b_filt_w2[l], LANE, LANE), _pad2(b_filt_b2[l][None], 1, LANE),
            _pad2(b_sin_freq[l][None], 1, LANE), _pad2(b_filt_w3[l], LANE, 4 * BRANCH_W), decay)
        pq = _spectra(fmat, hsd)
        z1 = _hyena_order(0, proj, COL_B, proj, b_short_conv[l], b_skip[l], fmat, gmat, pq, batch)
        z2 = _hyena_order(1, z1, 0, proj, b_short_conv[l], b_skip[l], fmat, gmat, pq, batch)

        yc = _pool(proj, c_group_w[l], c_scale[l], batch)
        y = _gate(oa, z2, yc, od, gate, a_out_norm[l], b_out_norm[l], batch)
        if l < depth - 1:
            h = _matmul(y, w_out, l, 0, D_MODEL, F32, residual=h)
        else:
            out = _matmul_final(y.reshape(batch, L_PAD, D_MODEL), w_out, l,
                                h.reshape(batch, L_PAD, D_MODEL))
    return out
```

```python
import functools
import math

import jax
import jax.numpy as jnp
from jax import lax
from jax.experimental import pallas as pl
from jax.experimental.pallas import tpu as pltpu

F32 = jnp.float32
BF16 = jnp.bfloat16

D_MODEL = 4096
N_META = 16
GRID_W = 64
EPS = 1e-6
BRANCH_W = 1024
HEAD_DIM = 128
A_HEADS = 8
A_KV_HEADS = 2
A_GROUPS = A_HEADS // A_KV_HEADS
AXIAL_THETA = 10000.0
HYENA_EMB_DIM = 33
HYENA_FFN = 64
HYENA_FAST = 0.3
HYENA_SLOW = 1.5
HYENA_TARGET = 1e-2
HYENA_SHIFT = 0.05
POOL_WINDOWS = (2, 4, 8, 16)
POOL_GROUP_W = 256
D_HEADS = 4
D_HEAD_V = 256
ROPE_THETA = 500000.0
ROPE_DIMS = 32

COL_AQ = 0
COL_AK = 1024
COL_AV = 1280
COL_B = 1536
COL_C = 4608
COL_DQ = 5632
COL_DK = 6656
COL_DV = 7680
COL_GATE = 8704
MAIN_W = COL_GATE
IN_COLS = 12800

LANE = 128
SEQ_REAL = 2048
L_TRUE = N_META + SEQ_REAL
L_PAD = 2176
DFT_N = 4607
N_FREQ = 2304
FREQ_CHUNK = 256
N_CHUNK = N_FREQ // FREQ_CHUNK
NEG_BIG = -1e30
LOG2E = 1.4426950408889634
Q_SCALE = (HEAD_DIM ** -0.5) * LOG2E
assert L_TRUE > L_PAD - LANE
VMEM_CAP = 56 * 1024 * 1024


def _cparams(semantics, vmem_bytes):
    return pltpu.CompilerParams(dimension_semantics=semantics,
                                vmem_limit_bytes=min(int(vmem_bytes), VMEM_CAP))


def _to_sequence_order(x):
    return jnp.concatenate([x[SEQ_REAL:L_TRUE], x[:SEQ_REAL], x[L_TRUE:]], axis=0)


def _store_from_sequence_order(o_ref, y, cols=slice(None)):
    o_ref[:SEQ_REAL, cols] = y[N_META:L_TRUE].astype(o_ref.dtype)
    o_ref[SEQ_REAL:L_TRUE, cols] = y[:N_META].astype(o_ref.dtype)
    o_ref[L_TRUE:, cols] = y[L_TRUE:].astype(o_ref.dtype)


def _rmsnorm_kernel(h_ref, w_ref, o_ref):
    x = h_ref[...]
    ms = jnp.mean(x * x, axis=-1, keepdims=True)
    o_ref[...] = (x * lax.rsqrt(ms + EPS) * w_ref[...]).astype(o_ref.dtype)


def _embed_rmsnorm_kernel(n_real_tiles, x_ref, tail_ref, w_ref, h_ref, u_ref):
    def emit(x):
        h_ref[...] = x
        ms = jnp.mean(x * x, axis=-1, keepdims=True)
        u_ref[...] = (x * lax.rsqrt(ms + EPS) * w_ref[...]).astype(u_ref.dtype)

    @pl.when(pl.program_id(1) < n_real_tiles)
    def _():
        emit(x_ref[...])

    @pl.when(pl.program_id(1) == n_real_tiles)
    def _():
        emit(tail_ref[...])


def _embed_rmsnorm(x, tail, w, tm=LANE):
    batch, n_real, d = x.shape
    n_real_tiles = n_real // tm
    row = lambda b, i: (b, i, 0)
    return pl.pallas_call(
        functools.partial(_embed_rmsnorm_kernel, n_real_tiles),
        grid=(batch, L_PAD // tm),
        in_specs=[pl.BlockSpec((None, tm, d), lambda b, i: (b, jnp.minimum(i, n_real_tiles - 1), 0)),
                  pl.BlockSpec((tm, d), lambda b, i: (0, 0)),
                  pl.BlockSpec((1, d), lambda b, i: (0, 0))],
        out_specs=[pl.BlockSpec((None, tm, d), row), pl.BlockSpec((None, tm, d), row)],
        out_shape=[jax.ShapeDtypeStruct((batch, L_PAD, d), F32),
                   jax.ShapeDtypeStruct((batch, L_PAD, d), BF16)],
        compiler_params=_cparams(("parallel", "arbitrary"), 32 << 20),
    )(x, tail, w.reshape(1, d))


def _rmsnorm(h, w, tm=272):
    m, d = h.shape
    return pl.pallas_call(
        _rmsnorm_kernel,
        grid=(m // tm,),
        in_specs=[pl.BlockSpec((tm, d), lambda i: (i, 0)),
                  pl.BlockSpec((1, d), lambda i: (0, 0))],
        out_specs=pl.BlockSpec((tm, d), lambda i: (i, 0)),
        out_shape=jax.ShapeDtypeStruct((m, d), BF16),
        compiler_params=_cparams(("parallel",), 32 << 20),
    )(h, w.reshape(1, d))


def _matmul_kernel(x_ref, w_ref, o_ref, wb_ref):
    @pl.when(pl.program_id(1) == 0)
    def _():
        wb_ref[...] = w_ref[...].astype(BF16)

    o_ref[...] = jnp.dot(x_ref[...], wb_ref[...],
                         preferred_element_type=F32).astype(o_ref.dtype)


def _matmul_res_kernel(x_ref, w_ref, r_ref, o_ref, wb_ref):
    @pl.when(pl.program_id(1) == 0)
    def _():
        wb_ref[...] = w_ref[...].astype(BF16)

    o_ref[...] = r_ref[...] + jnp.dot(x_ref[...], wb_ref[...], preferred_element_type=F32)


def _matmul(x, w3, layer, col0, width, out_dtype, residual=None, tm=1088, tn=512):
    m, k = x.shape
    cb0 = col0 // tn
    grid = (width // tn, m // tm)
    in_specs = [pl.BlockSpec((tm, k), lambda j, i: (i, 0)),
                pl.BlockSpec((None, k, tn), lambda j, i: (layer, 0, cb0 + j))]
    args = [x, w3]
    kern = _matmul_kernel
    if residual is not None:
        in_specs.append(pl.BlockSpec((tm, tn), lambda j, i: (i, j)))
        args.append(residual)
        kern = _matmul_res_kernel
    vmem = 2 * (tm * k * 2 + k * tn * 4 + 2 * tm * tn * 4) + k * tn * 2 + (4 << 20)
    return pl.pallas_call(
        kern,
        grid=grid,
        in_specs=in_specs,
        out_specs=pl.BlockSpec((tm, tn), lambda j, i: (i, j)),
        out_shape=jax.ShapeDtypeStruct((m, width), out_dtype),
        scratch_shapes=[pltpu.VMEM((k, tn), BF16)],
        compiler_params=_cparams(("parallel", "arbitrary"), vmem),
    )(*args)


def _matmul_final_kernel(x_ref, w_ref, r_ref, o_ref, wb_ref):
    @pl.when((pl.program_id(1) == 0) & (pl.program_id(2) == 0))
    def _():
        wb_ref[...] = w_ref[...].astype(BF16)

    o_ref[...] = r_ref[...] + jnp.dot(x_ref[...], wb_ref[...], preferred_element_type=F32)


def _matmul_final(y3, w3, layer, h3, tm=1024, tn=512):
    batch, _, k = y3.shape
    d = h3.shape[-1]
    vmem = 2 * (tm * k * 2 + k * tn * 4 + 2 * tm * tn * 4) + k * tn * 2 + (4 << 20)
    return pl.pallas_call(
        _matmul_final_kernel,
        grid=(d // tn, batch, SEQ_REAL // tm),
        in_specs=[pl.BlockSpec((None, tm, k), lambda j, b, i: (b, i, 0)),
                  pl.BlockSpec((None, k, tn), lambda j, b, i: (layer, 0, j)),
                  pl.BlockSpec((None, tm, tn), lambda j, b, i: (b, i, j))],
        out_specs=pl.BlockSpec((None, tm, tn), lambda j, b, i: (b, i, j)),
        out_shape=jax.ShapeDtypeStruct((batch, SEQ_REAL, d), F32),
        scratch_shapes=[pltpu.VMEM((k, tn), BF16)],
        compiler_params=_cparams(("parallel", "arbitrary", "arbitrary"), vmem),
    )(y3, w3, h3)


def _norm_rope(x, gain, cos, sin_signed, half_rot):
    ms = jnp.mean(x * x, axis=-1, keepdims=True)
    xn = x * lax.rsqrt(ms + EPS) * gain
    lane = lax.broadcasted_iota(jnp.int32, (1, LANE), 1)
    partner_up = (lane & half_rot) == 0
    partner = jnp.where(partner_up,
                        pltpu.roll(xn, LANE - half_rot, 1),
                        pltpu.roll(xn, half_rot, 1))
    return xn * cos + partner * sin_signed


def _softmax_pv(q_bf, k_bf, v_bf, bias_last, denom_from_ones):
    s = lax.dot_general(q_bf, k_bf, (((1,), (1,)), ((), ())), preferred_element_type=F32)
    n_main = L_PAD - LANE
    s_main = s[:, :n_main]
    s_last = s[:, n_main:] + bias_last
    m = jnp.maximum(jnp.max(s_main, axis=-1, keepdims=True),
                    jnp.max(s_last, axis=-1, keepdims=True))
    p_main = jnp.exp2(s_main - m)
    p_last = jnp.exp2(s_last - m)
    p = jnp.concatenate([p_main.astype(BF16), p_last.astype(BF16)], axis=1)
    o = jnp.dot(p, v_bf, preferred_element_type=F32)
    if denom_from_ones:
        return o[:, :HEAD_DIM] * (1.0 / o[:, HEAD_DIM:HEAD_DIM + 1])
    l = jnp.sum(p_main, axis=-1, keepdims=True) + jnp.sum(p_last, axis=-1, keepdims=True)
    return o * (1.0 / l)


def _attn_a_kernel(sub, q_ref, k_ref, v_ref, cq_ref, sq_ref, ck_ref, sk_ref, gq_ref, gk_ref,
                   bias_ref, o_ref, kh_ref, vx_ref, qh_ref):
    @pl.when((pl.program_id(2) == 0) & (pl.program_id(3) == 0))
    def _():
        kh_ref[...] = _norm_rope(k_ref[...].astype(F32), gk_ref[...], ck_ref[...], sk_ref[...],
                                 HEAD_DIM // 4).astype(BF16)
        vx_ref[:, :HEAD_DIM] = v_ref[...]
        vx_ref[:, HEAD_DIM:] = jnp.ones((L_PAD, HEAD_DIM), BF16)

    q = _norm_rope(q_ref[...].astype(F32), gq_ref[...], cq_ref[...], sq_ref[...], HEAD_DIM // 4)
    qh_ref[...] = (q * Q_SCALE).astype(BF16)
    bias_last = bias_ref[:, L_PAD - LANE:]
    for i in range(q_ref.shape[0] // sub):
        rows = slice(i * sub, (i + 1) * sub)
        o_ref[rows, :] = _softmax_pv(qh_ref[rows, :], kh_ref[...], vx_ref[...], bias_last,
                                     True).astype(o_ref.dtype)


def _attn_a(proj, cos_t, sin_t, gq, gk, bias, batch, tq=L_PAD, sub=272):
    m = proj.shape[0]
    nq = L_PAD // tq
    qblk = lambda b, kv, g, qi: (b * nq + qi, COL_AQ // HEAD_DIM + kv * A_GROUPS + g)
    tabq = lambda b, kv, g, qi: (qi, 0)
    full = lambda b, kv, g, qi: (0, 0)
    return pl.pallas_call(
        functools.partial(_attn_a_kernel, sub),
        grid=(batch, A_KV_HEADS, A_GROUPS, nq),
        in_specs=[
            pl.BlockSpec((tq, HEAD_DIM), qblk),
            pl.BlockSpec((L_PAD, HEAD_DIM), lambda b, kv, g, qi: (b, COL_AK // HEAD_DIM + kv)),
            pl.BlockSpec((L_PAD, HEAD_DIM), lambda b, kv, g, qi: (b, COL_AV // HEAD_DIM + kv)),
            pl.BlockSpec((tq, HEAD_DIM), tabq),
            pl.BlockSpec((tq, HEAD_DIM), tabq),
            pl.BlockSpec((L_PAD, HEAD_DIM), full),
            pl.BlockSpec((L_PAD, HEAD_DIM), full),
            pl.BlockSpec((1, HEAD_DIM), full),
            pl.BlockSpec((1, HEAD_DIM), full),
            pl.BlockSpec((1, L_PAD), full),
        ],
        out_specs=pl.BlockSpec((tq, HEAD_DIM), lambda b, kv, g, qi: (b * nq + qi, kv * A_GROUPS + g)),
        out_shape=jax.ShapeDtypeStruct((m, BRANCH_W), BF16),
        scratch_shapes=[pltpu.VMEM((L_PAD, HEAD_DIM), BF16),
                        pltpu.VMEM((L_PAD, 2 * HEAD_DIM), BF16),
                        pltpu.VMEM((tq, HEAD_DIM), BF16)],
        compiler_params=_cparams(("parallel", "parallel", "arbitrary", "arbitrary"), 48 << 20),
    )(proj, proj, proj, cos_t, sin_t, cos_t, sin_t, gq.reshape(1, -1), gk.reshape(1, -1), bias)


def _attn_d_kernel(lam_init, sub, q_ref, k_ref, v_ref, cq_ref, sq_ref, ck_ref, sk_ref, gq_ref,
                   gk_ref, lam_ref, gout_ref, bias_ref, o_ref, kh_ref, qh_ref):
    halves = (slice(0, HEAD_DIM), slice(HEAD_DIM, 2 * HEAD_DIM))

    @pl.when(pl.program_id(2) == 0)
    def _():
        for sl in halves:
            kh_ref[:, sl] = _norm_rope(k_ref[:, sl].astype(F32), gk_ref[...], ck_ref[...],
                                       sk_ref[...], ROPE_DIMS // 2).astype(BF16)

    for sl in halves:
        q = _norm_rope(q_ref[:, sl].astype(F32), gq_ref[...], cq_ref[...], sq_ref[...],
                       ROPE_DIMS // 2)
        qh_ref[:, sl] = (q * Q_SCALE).astype(BF16)
    lf = lam_ref[...]
    lam = (jnp.exp(jnp.sum(lf[0:1] * lf[1:2], axis=-1, keepdims=True))
           - jnp.exp(jnp.sum(lf[2:3] * lf[3:4], axis=-1, keepdims=True)) + lam_init)
    bias_last = bias_ref[:, L_PAD - LANE:]
    for i in range(q_ref.shape[0] // sub):
        rows = slice(i * sub, (i + 1) * sub)
        o1, o2 = [_softmax_pv(qh_ref[rows, sl], kh_ref[:, sl], v_ref[...], bias_last, False)
                  for sl in halves]
        o = o1 - lam * o2
        ms = jnp.mean(o * o, axis=-1, keepdims=True)
        o = o * lax.rsqrt(ms + EPS) * gout_ref[...] * (1.0 - lam_init)
        o_ref[rows, :] = o.astype(o_ref.dtype)


def _attn_d(proj, cos_t, sin_t, gq, gk, lam_vec, gout, bias, lam_init, batch, tq=L_PAD, sub=272):
    m = proj.shape[0]
    nq = L_PAD // tq
    w = 2 * HEAD_DIM
    tabq = lambda b, h, qi: (qi, 0)
    full = lambda b, h, qi: (0, 0)
    return pl.pallas_call(
        functools.partial(_attn_d_kernel, lam_init, sub),
        grid=(batch, D_HEADS, nq),
        in_specs=[
            pl.BlockSpec((tq, w), lambda b, h, qi: (b * nq + qi, COL_DQ // w + h)),
            pl.BlockSpec((L_PAD, w), lambda b, h, qi: (b, COL_DK // w + h)),
            pl.BlockSpec((L_PAD, w), lambda b, h, qi: (b, COL_DV // w + h)),
            pl.BlockSpec((tq, HEAD_DIM), tabq),
            pl.BlockSpec((tq, HEAD_DIM), tabq),
            pl.BlockSpec((L_PAD, HEAD_DIM), full),
            pl.BlockSpec((L_PAD, HEAD_DIM), full),
            pl.BlockSpec((1, HEAD_DIM), full),
            pl.BlockSpec((1, HEAD_DIM), full),
            pl.BlockSpec((4, HEAD_DIM), full),
            pl.BlockSpec((1, w), full),
            pl.BlockSpec((1, L_PAD), full),
        ],
        out_specs=pl.BlockSpec((tq, w), lambda b, h, qi: (b * nq + qi, h)),
        out_shape=jax.ShapeDtypeStruct((m, BRANCH_W), BF16),
        scratch_shapes=[pltpu.VMEM((L_PAD, w), BF16), pltpu.VMEM((tq, w), BF16)],
        compiler_params=_cparams(("parallel", "parallel", "arbitrary"), 48 << 20),
    )(proj, proj, proj, cos_t, sin_t, cos_t, sin_t, gq.reshape(1, -1), gk.reshape(1, -1),
      lam_vec, gout.reshape(1, -1), bias)


def _hyena_filter_kernel(z_ref, w1_ref, b1_ref, w2_ref, b2_ref, fr_ref, w3f_ref, w3b_ref,
                         dec_ref, o_ref, hid_ref):
    hp = lax.Precision.HIGHEST

    @pl.when((pl.program_id(0) == 0) & (pl.program_id(1) == 0))
    def _():
        fr = fr_ref[...]
        h1 = jnp.sin(fr * (jnp.dot(z_ref[...], w1_ref[...], precision=hp,
                                   preferred_element_type=F32) + b1_ref[...]))
        hid_ref[...] = jnp.sin(fr * (jnp.dot(h1, w2_ref[...], precision=hp,
                                             preferred_element_type=F32) + b2_ref[...]))

    hid = hid_ref[...]
    dec = dec_ref[...]
    lag = lax.broadcasted_iota(jnp.int32, (L_PAD, 1), 0)
    hf = jnp.dot(hid, w3f_ref[...], precision=hp, preferred_element_type=F32) * dec
    hb = jnp.dot(hid, w3b_ref[...], precision=hp, preferred_element_type=F32) * dec
    hf = jnp.where(lag < L_TRUE, hf, 0.0)
    hb = jnp.where((lag >= 1) & (lag < L_TRUE), hb, 0.0)
    o_ref[0] = (hf + hb).astype(o_ref.dtype)
    o_ref[1] = (hf - hb).astype(o_ref.dtype)


def _hyena_filter(zfeat, w1, b1, w2, b2, fr, w3, decay, tc=512):
    nct = BRANCH_W // tc
    full = lambda o, c: (0, 0)
    return pl.pallas_call(
        _hyena_filter_kernel,
        grid=(2, nct),
        in_specs=[
            pl.BlockSpec((L_PAD, LANE), full),
            pl.BlockSpec((LANE, LANE), full),
            pl.BlockSpec((1, LANE), full),
            pl.BlockSpec((LANE, LANE), full),
            pl.BlockSpec((1, LANE), full),
            pl.BlockSpec((1, LANE), full),
            pl.BlockSpec((LANE, tc), lambda o, c: (0, (o * 2) * nct + c)),
            pl.BlockSpec((LANE, tc), lambda o, c: (0, (o * 2 + 1) * nct + c)),
            pl.BlockSpec((L_PAD, tc), lambda o, c: (0, c)),
        ],
        out_specs=pl.BlockSpec((2, L_PAD, tc), lambda o, c: (0, 0, o * nct + c)),
        out_shape=jax.ShapeDtypeStruct((2, L_PAD, 2 * BRANCH_W), BF16),
        scratch_shapes=[pltpu.VMEM((L_PAD, LANE), F32)],
        compiler_params=_cparams(("arbitrary", "arbitrary"), 48 << 20),
    )(zfeat, w1, b1, w2, b2, fr, w3, w3, decay)


def _dft_tables_kernel(bc_ref, bs_ref, btc_ref, bts_ref, sc_ref, ss_ref, sct_ref, sst_ref,
                       f_ref, g_ref):
    sc, ss = sc_ref[...], ss_ref[...]
    bc, bs = bc_ref[...], bs_ref[...]
    f_ref[:FREQ_CHUNK, :] = (sc * bc - ss * bs).astype(f_ref.dtype)
    f_ref[FREQ_CHUNK:, :] = (ss * bc + sc * bs).astype(f_ref.dtype)
    sct, sst = sct_ref[...], sst_ref[...]
    btc, bts = btc_ref[...], bts_ref[...]
    g_ref[:, :FREQ_CHUNK] = (sct * btc - sst * bts).astype(g_ref.dtype)
    g_ref[:, FREQ_CHUNK:] = (sst * btc + sct * bts).astype(g_ref.dtype)


def _dft_tables():
    def trig(k, t):
        ang = ((k * t) % DFT_N).astype(F32) * (2.0 * math.pi / DFT_N)
        return jnp.cos(ang), jnp.sin(ang)

    t = jnp.arange(L_PAD, dtype=jnp.int32)
    bc, bs = trig(jnp.arange(FREQ_CHUNK, dtype=jnp.int32)[:, None], t[None, :])
    live = (t < L_TRUE).astype(F32)[None, :]
    sc, ss = trig((jnp.arange(N_CHUNK, dtype=jnp.int32) * FREQ_CHUNK)[:, None], t[None, :])
    sc, ss = sc * live, ss * live
    full2 = lambda j: (0, 0)
    step_row = pl.BlockSpec((None, 1, L_PAD), lambda j: (j, 0, 0))
    step_col = pl.BlockSpec((None, L_PAD, 1), lambda j: (j, 0, 0))
    return pl.pallas_call(
        _dft_tables_kernel,
        grid=(N_CHUNK,),
        in_specs=[pl.BlockSpec((FREQ_CHUNK, L_PAD), full2), pl.BlockSpec((FREQ_CHUNK, L_PAD), full2),
                  pl.BlockSpec((L_PAD, FREQ_CHUNK), full2), pl.BlockSpec((L_PAD, FREQ_CHUNK), full2),
                  step_row, step_row, step_col, step_col],
        out_specs=[pl.BlockSpec((None, 2 * FREQ_CHUNK, L_PAD), lambda j: (j, 0, 0)),
                   pl.BlockSpec((None, L_PAD, 2 * FREQ_CHUNK), lambda j: (j, 0, 0))],
        out_shape=[jax.ShapeDtypeStruct((N_CHUNK, 2 * FREQ_CHUNK, L_PAD), BF16),
                   jax.ShapeDtypeStruct((N_CHUNK, L_PAD, 2 * FREQ_CHUNK), BF16)],
        compiler_params=_cparams(("parallel",), 48 << 20),
    )(bc, bs, bc.T, bs.T, sc[:, None, :], ss[:, None, :], sc[:, :, None], ss[:, :, None])


def _spectra_kernel(f_ref, h_ref, o_ref):
    k = pl.program_id(1) * FREQ_CHUNK + lax.broadcasted_iota(jnp.int32, (FREQ_CHUNK, 1), 0)
    wk = jnp.where(k == 0, 1.0 / DFT_N, 2.0 / DFT_N)
    o_ref[:FREQ_CHUNK, :] = jnp.dot(f_ref[:FREQ_CHUNK, :], h_ref[0],
                                    preferred_element_type=F32) * wk
    o_ref[FREQ_CHUNK:, :] = jnp.dot(f_ref[FREQ_CHUNK:, :], h_ref[1],
                                    preferred_element_type=F32) * wk


def _spectra(fmat, hsd, tn=1024):
    ncol = hsd.shape[-1]
    return pl.pallas_call(
        _spectra_kernel,
        grid=(ncol // tn, N_CHUNK),
        in_specs=[pl.BlockSpec((None, 2 * FREQ_CHUNK, L_PAD), lambda n, f: (f, 0, 0)),
                  pl.BlockSpec((2, L_PAD, tn), lambda n, f: (0, 0, n))],
        out_specs=pl.BlockSpec((None, 2 * FREQ_CHUNK, tn), lambda n, f: (f, 0, n)),
        out_shape=jax.ShapeDtypeStruct((N_CHUNK, 2 * FREQ_CHUNK, ncol), F32),
        compiler_params=_cparams(("parallel", "arbitrary"), 40 << 20),
    )(fmat, hsd)


def _short_conv3(x, w):
    return (pltpu.roll(x, 1, 0) * w[0:1] + x * w[1:2] + pltpu.roll(x, L_PAD - 1, 0) * w[2:3])


def _hyena_order_kernel(first, z_ref, g_ref, wz_ref, wg_ref, skip_ref, f_ref, gm_ref, pq_ref,
                        o_ref, zs_ref, zf_ref, acc_ref):
    fk = pl.program_id(2)

    @pl.when(fk == 0)
    def _():
        z = z_ref[...].astype(F32)
        if first:
            z = _short_conv3(_to_sequence_order(z), wz_ref[...])
        zf_ref[...] = z
        zs_ref[...] = z.astype(BF16)
        acc_ref[...] = jnp.zeros_like(acc_ref)

    ab = jnp.dot(f_ref[...], zs_ref[...], preferred_element_type=F32)
    a, b = ab[:FREQ_CHUNK], ab[FREQ_CHUNK:]
    p, q = pq_ref[:FREQ_CHUNK, :], pq_ref[FREQ_CHUNK:, :]
    uv = jnp.concatenate([(a * p - b * q).astype(BF16), (a * q + b * p).astype(BF16)], axis=0)
    acc_ref[...] += jnp.dot(gm_ref[...], uv, preferred_element_type=F32)

    @pl.when(fk == pl.num_programs(2) - 1)
    def _():
        gate = _short_conv3(_to_sequence_order(g_ref[...].astype(F32)), wg_ref[...])
        y = gate * (acc_ref[...] + skip_ref[...] * zf_ref[...])
        if first:
            o_ref[...] = y
        else:
            _store_from_sequence_order(o_ref, y)


def _hyena_order(order, zin, zin_col0, proj, conv_w, skip, fmat, gmat, pq, batch, tc=512):
    first = order == 0
    m = proj.shape[0]
    nct = BRANCH_W // tc
    zcb = zin_col0 // tc
    gcb = (COL_B + (order + 1) * BRANCH_W) // tc
    in_specs = [
        pl.BlockSpec((L_PAD, tc), lambda b, c, f: (b, zcb + c)),
        pl.BlockSpec((L_PAD, tc), lambda b, c, f: (b, gcb + c)),
        pl.BlockSpec((3, tc), lambda b, c, f: (0, c)),
        pl.BlockSpec((3, tc), lambda b, c, f: (0, (order + 1) * nct + c)),
        pl.BlockSpec((1, tc), lambda b, c, f: (0, c)),
        pl.BlockSpec((None, 2 * FREQ_CHUNK, L_PAD), lambda b, c, f: (f, 0, 0)),
        pl.BlockSpec((None, L_PAD, 2 * FREQ_CHUNK), lambda b, c, f: (f, 0, 0)),
        pl.BlockSpec((None, 2 * FREQ_CHUNK, tc), lambda b, c, f: (f, 0, order * nct + c)),
    ]
    return pl.pallas_call(
        functools.partial(_hyena_order_kernel, first),
        grid=(batch, nct, N_CHUNK),
        in_specs=in_specs,
        out_specs=pl.BlockSpec((L_PAD, tc), lambda b, c, f: (b, c)),
        out_shape=jax.ShapeDtypeStruct((m, BRANCH_W), F32),
        scratch_shapes=[pltpu.VMEM((L_PAD, tc), BF16),
                        pltpu.VMEM((L_PAD, tc), F32),
                        pltpu.VMEM((L_PAD, tc), F32)],
        compiler_params=_cparams(("parallel", "parallel", "arbitrary"), VMEM_CAP),
    )(zin, proj, conv_w, conv_w, skip[order].reshape(1, BRANCH_W), fmat, gmat, pq)


def _pool_kernel(p0_ref, p1_ref, p2_ref, p3_ref, gw_ref, scale_ref, o_ref):
    t = lax.broadcasted_iota(jnp.int32, (L_PAD, 1), 0)
    for g, (w, p_ref) in enumerate(zip(POOL_WINDOWS, (p0_ref, p1_ref, p2_ref, p3_ref))):
        p = _to_sequence_order(p_ref[...].astype(F32))
        s = p
        span = 1
        while span < w:
            s = s + pltpu.roll(s, L_PAD - span, 0)
            span *= 2
        back = (w - 1) // 2
        if back:
            s = pltpu.roll(s, back, 0)
        lo = jnp.clip(t - back, 0, L_TRUE)
        hi = jnp.clip(t + w // 2 + 1, 0, L_TRUE)
        cnt = jnp.maximum(hi - lo, 1).astype(F32)
        d = (s / cnt - p).astype(BF16)
        y = jnp.dot(d, gw_ref[g].astype(BF16), preferred_element_type=F32)
        sl = slice(g * POOL_GROUP_W, (g + 1) * POOL_GROUP_W)
        _store_from_sequence_order(o_ref, y * scale_ref[:, sl], sl)


def _pool(proj, group_w, scale, batch):
    m = proj.shape[0]
    cb = COL_C // POOL_GROUP_W
    grp = lambda g: pl.BlockSpec((L_PAD, POOL_GROUP_W), lambda b: (b, cb + g))
    return pl.pallas_call(
        _pool_kernel,
        grid=(batch,),
        in_specs=[grp(0), grp(1), grp(2), grp(3),
                  pl.BlockSpec((4, POOL_GROUP_W, POOL_GROUP_W), lambda b: (0, 0, 0)),
                  pl.BlockSpec((1, BRANCH_W), lambda b: (0, 0))],
        out_specs=pl.BlockSpec((L_PAD, BRANCH_W), lambda b: (b, 0)),
        out_shape=jax.ShapeDtypeStruct((m, BRANCH_W), BF16),
        compiler_params=_cparams(("parallel",), 48 << 20),
    )(proj, proj, proj, proj, group_w, scale.reshape(1, -1))


GATE_ROWS = 16


def _gate_kernel(tm, oa_ref, zb_ref, yc_ref, od_ref, gate_ref, na_ref, nb_ref, y_ref):
    branches = ((oa_ref, na_ref), (zb_ref, nb_ref), (yc_ref, None), (od_ref, None))

    def row_group(r, carry):
        row0 = pl.multiple_of(r * GATE_ROWS, GATE_ROWS)
        rows = pl.ds(row0, GATE_ROWS)
        pos = (pl.program_id(1) * tm + row0
               + lax.broadcasted_iota(jnp.int32, (GATE_ROWS, 1), 0))
        live = pos < L_TRUE
        for i, (x_ref, norm_ref) in enumerate(branches):
            sl = slice(i * BRANCH_W, (i + 1) * BRANCH_W)
            y = x_ref[rows, :].astype(F32)
            if norm_ref is not None:
                ms = jnp.mean(y * y, axis=-1, keepdims=True)
                y = y * lax.rsqrt(ms + EPS) * norm_ref[...]
            g = gate_ref[rows, sl].astype(F32)
            y = y * (g * jax.nn.sigmoid(g))
            y_ref[rows, sl] = jnp.where(live, y, 0.0).astype(y_ref.dtype)
        return carry

    lax.fori_loop(0, tm // GATE_ROWS, row_group, 0)


def _gate(oa, zb, yc, od, gate, na, nb, batch, tm=272):
    m = oa.shape[0]
    nt = L_PAD // tm
    row = lambda b, t: (b * nt + t, 0)
    full = lambda b, t: (0, 0)
    br = pl.BlockSpec((tm, BRANCH_W), row)
    return pl.pallas_call(
        functools.partial(_gate_kernel, tm),
        grid=(batch, nt),
        in_specs=[br, br, br, br, pl.BlockSpec((tm, D_MODEL), row),
                  pl.BlockSpec((1, BRANCH_W), full), pl.BlockSpec((1, BRANCH_W), full)],
        out_specs=pl.BlockSpec((tm, D_MODEL), row),
        out_shape=jax.ShapeDtypeStruct((m, D_MODEL), BF16),
        compiler_params=_cparams(("parallel", "parallel"), 32 << 20),
    )(oa, zb, yc, od, gate, na.reshape(1, -1), nb.reshape(1, -1))


def _inv_freq(dim, theta):
    return theta ** (-jnp.arange(0, dim, 2, dtype=F32) / dim)


def _pad_rows(t):
    return jnp.pad(t, ((0, L_PAD - t.shape[0]), (0, 0)))


def _rope_tables():
    rows = SEQ_REAL // GRID_W
    zeros_meta = jnp.zeros((N_META,), F32)
    row = jnp.concatenate([zeros_meta, jnp.repeat(jnp.arange(rows, dtype=F32), GRID_W)])
    col = jnp.concatenate([zeros_meta, jnp.tile(jnp.arange(GRID_W, dtype=F32), rows)])
    axf = _inv_freq(HEAD_DIM // 2, AXIAL_THETA)
    ar = row[:, None] * axf[None, :]
    ac = col[:, None] * axf[None, :]
    cos_a = jnp.concatenate([jnp.cos(ar), jnp.cos(ar), jnp.cos(ac), jnp.cos(ac)], axis=-1)
    sin_a = jnp.concatenate([-jnp.sin(ar), jnp.sin(ar), -jnp.sin(ac), jnp.sin(ac)], axis=-1)
    a1 = jnp.arange(L_TRUE, dtype=F32)[:, None] * _inv_freq(ROPE_DIMS, ROPE_THETA)[None, :]
    rest = HEAD_DIM - ROPE_DIMS
    cos_d = jnp.concatenate([jnp.cos(a1), jnp.cos(a1), jnp.ones((L_TRUE, rest), F32)], axis=-1)
    sin_d = jnp.concatenate([-jnp.sin(a1), jnp.sin(a1), jnp.zeros((L_TRUE, rest), F32)], axis=-1)
    def hbm_rows(tab):
        return _pad_rows(jnp.concatenate([tab[N_META:], tab[:N_META]], axis=0))

    return hbm_rows(cos_a), hbm_rows(sin_a), hbm_rows(cos_d), hbm_rows(sin_d)


def _hyena_tables():
    t = jnp.linspace(0.0, 1.0, L_TRUE, dtype=F32)[:, None]
    bands = (HYENA_EMB_DIM - 1) // 2
    fb = jnp.linspace(1e-4, bands - 1, bands, dtype=F32)[None, :]
    wpos = 2.0 * math.pi * jnp.arange(L_TRUE, dtype=F32)[:, None] / L_TRUE
    z = jnp.concatenate([t, jnp.cos(fb * wpos), -jnp.sin(fb * wpos)], axis=-1)
    z = jnp.pad(z, ((0, L_PAD - L_TRUE), (0, LANE - HYENA_EMB_DIM)))
    deltas = jnp.abs(jnp.linspace(math.log(HYENA_TARGET) / HYENA_SLOW,
                                  math.log(HYENA_TARGET) / HYENA_FAST, BRANCH_W, dtype=F32))
    decay = _pad_rows(jnp.exp(-t * deltas) + HYENA_SHIFT)
    return z, decay


def _pad2(a, rows, cols):
    return jnp.pad(a, ((0, rows - a.shape[0]), (0, cols - a.shape[1])))


def kernel(x, meta_tokens, norm_w, w_in, w_out, a_q_norm, a_k_norm, a_out_norm, b_short_conv,
           b_filt_w1, b_filt_b1, b_filt_w2, b_filt_b2, b_filt_w3, b_sin_freq, b_skip, b_out_norm,
           c_group_w, c_scale, d_q_norm, d_k_norm, d_lambda, d_out_norm):
    batch = x.shape[0]
    depth = norm_w.shape[0]
    m = batch * L_PAD
    tail = _pad2(meta_tokens.astype(x.dtype), L_PAD - SEQ_REAL, D_MODEL)

    cos_a, sin_a, cos_d, sin_d = _rope_tables()
    fmat, gmat = _dft_tables()
    zfeat, decay = _hyena_tables()
    bias = jnp.where(jnp.arange(L_PAD) < L_TRUE, 0.0, NEG_BIG).astype(F32).reshape(1, L_PAD)

    for l in range(depth):
        lam_init = 0.8 - 0.6 * math.exp(-0.3 * l)
        if l == 0:
            h, u = _embed_rmsnorm(x, tail, norm_w[0])
            h, u = h.reshape(m, D_MODEL), u.reshape(m, D_MODEL)
        else:
            u = _rmsnorm(h, norm_w[l])
        proj = _matmul(u, w_in, l, 0, MAIN_W, BF16)
        gate = _matmul(u, w_in, l, COL_GATE, D_MODEL, BF16)

        oa = _attn_a(proj, cos_a, sin_a, a_q_norm[l], a_k_norm[l], bias, batch)
        od = _attn_d(proj, cos_d, sin_d, d_q_norm[l], d_k_norm[l], d_lambda[l], d_out_norm[l],
                     bias, lam_init, batch)

        hsd = _hyena_filter(
            zfeat,
            _pad2(b_filt_w1[l], LANE, LANE), _pad2(b_filt_b1[l][None], 1, LANE),
            _pad2(b_filt_w2[l], LANE, LANE), _pad2(b_filt_b2[l][None], 1, LANE),
            _pad2(b_sin_freq[l][None], 1, LANE), _pad2(b_filt_w3[l], LANE, 4 * BRANCH_W), decay)
        pq = _spectra(fmat, hsd)
        z1 = _hyena_order(0, proj, COL_B, proj, b_short_conv[l], b_skip[l], fmat, gmat, pq, batch)
        z2 = _hyena_order(1, z1, 0, proj, b_short_conv[l], b_skip[l], fmat, gmat, pq, batch)

        yc = _pool(proj, c_group_w[l], c_scale[l], batch)
        y = _gate(oa, z2, yc, od, gate, a_out_norm[l], b_out_norm[l], batch)
        if l < depth - 1:
            h = _matmul(y, w_out, l, 0, D_MODEL, F32, residual=h)
        else:
            out = _matmul_final(y.reshape(batch, L_PAD, D_MODEL), w_out, l,
                                h.reshape(batch, L_PAD, D_MODEL))
    return out
```

```python
import functools
import math

import jax
import jax.numpy as jnp
from jax import lax
from jax.experimental import pallas as pl
from jax.experimental.pallas import tpu as pltpu

F32 = jnp.float32
BF16 = jnp.bfloat16

D_MODEL = 4096
N_META = 16
GRID_W = 64
EPS = 1e-6
BRANCH_W = 1024
HEAD_DIM = 128
A_HEADS = 8
A_KV_HEADS = 2
A_GROUPS = A_HEADS // A_KV_HEADS
AXIAL_THETA = 10000.0
HYENA_EMB_DIM = 33
HYENA_FFN = 64
HYENA_FAST = 0.3
HYENA_SLOW = 1.5
HYENA_TARGET = 1e-2
HYENA_SHIFT = 0.05
POOL_WINDOWS = (2, 4, 8, 16)
POOL_GROUP_W = 256
D_HEADS = 4
D_HEAD_V = 256
ROPE_THETA = 500000.0
ROPE_DIMS = 32

COL_AQ = 0
COL_AK = 1024
COL_AV = 1280
COL_B = 1536
COL_C = 4608
COL_DQ = 5632
COL_DK = 6656
COL_DV = 7680
COL_GATE = 8704
MAIN_W = COL_GATE
IN_COLS = 12800

LANE = 128
SEQ_REAL = 2048
L_TRUE = N_META + SEQ_REAL
L_PAD = 2176
DFT_N = 4607
N_FREQ = 2304
FREQ_CHUNK = 256
N_CHUNK = N_FREQ // FREQ_CHUNK
NEG_BIG = -1e30
LOG2E = 1.4426950408889634
Q_SCALE = (HEAD_DIM ** -0.5) * LOG2E
assert L_TRUE > L_PAD - LANE
VMEM_CAP = 56 * 1024 * 1024


def _cparams(semantics, vmem_bytes):
    return pltpu.CompilerParams(dimension_semantics=semantics,
                                vmem_limit_bytes=min(int(vmem_bytes), VMEM_CAP))


def _to_sequence_order(x):
    return jnp.concatenate([x[SEQ_REAL:L_TRUE], x[:SEQ_REAL], x[L_TRUE:]], axis=0)


def _store_from_sequence_order(o_ref, y, cols=slice(None)):
    o_ref[:SEQ_REAL, cols] = y[N_META:L_TRUE].astype(o_ref.dtype)
    o_ref[SEQ_REAL:L_TRUE, cols] = y[:N_META].astype(o_ref.dtype)
    o_ref[L_TRUE:, cols] = y[L_TRUE:].astype(o_ref.dtype)


def _rmsnorm_kernel(h_ref, w_ref, o_ref):
    x = h_ref[...]
    ms = jnp.mean(x * x, axis=-1, keepdims=True)
    o_ref[...] = (x * lax.rsqrt(ms + EPS) * w_ref[...]).astype(o_ref.dtype)


def _embed_rmsnorm_kernel(n_real_tiles, x_ref, tail_ref, w_ref, h_ref, u_ref):
    def emit(x):
        h_ref[...] = x
        ms = jnp.mean(x * x, axis=-1, keepdims=True)
        u_ref[...] = (x * lax.rsqrt(ms + EPS) * w_ref[...]).astype(u_ref.dtype)

    @pl.when(pl.program_id(1) < n_real_tiles)
    def _():
        emit(x_ref[...])

    @pl.when(pl.program_id(1) == n_real_tiles)
    def _():
        emit(tail_ref[...])


def _embed_rmsnorm(x, tail, w, tm=LANE):
    batch, n_real, d = x.shape
    n_real_tiles = n_real // tm
    row = lambda b, i: (b, i, 0)
    return pl.pallas_call(
        functools.partial(_embed_rmsnorm_kernel, n_real_tiles),
        grid=(batch, L_PAD // tm),
        in_specs=[pl.BlockSpec((None, tm, d), lambda b, i: (b, jnp.minimum(i, n_real_tiles - 1), 0)),
                  pl.BlockSpec((tm, d), lambda b, i: (0, 0)),
                  pl.BlockSpec((1, d), lambda b, i: (0, 0))],
        out_specs=[pl.BlockSpec((None, tm, d), row), pl.BlockSpec((None, tm, d), row)],
        out_shape=[jax.ShapeDtypeStruct((batch, L_PAD, d), F32),
                   jax.ShapeDtypeStruct((batch, L_PAD, d), BF16)],
        compiler_params=_cparams(("parallel", "arbitrary"), 32 << 20),
    )(x, tail, w.reshape(1, d))


def _rmsnorm(h, w, tm=272):
    m, d = h.shape
    return pl.pallas_call(
        _rmsnorm_kernel,
        grid=(m // tm,),
        in_specs=[pl.BlockSpec((tm, d), lambda i: (i, 0)),
                  pl.BlockSpec((1, d), lambda i: (0, 0))],
        out_specs=pl.BlockSpec((tm, d), lambda i: (i, 0)),
        out_shape=jax.ShapeDtypeStruct((m, d), BF16),
        compiler_params=_cparams(("parallel",), 32 << 20),
    )(h, w.reshape(1, d))


def _matmul_kernel(silu, x_ref, w_ref, o_ref, wb_ref):
    @pl.when(pl.program_id(1) == 0)
    def _():
        wb_ref[...] = w_ref[...].astype(BF16)

    acc = jnp.dot(x_ref[...], wb_ref[...], preferred_element_type=F32)
    if silu:
        acc = acc * jax.nn.sigmoid(acc)
    o_ref[...] = acc.astype(o_ref.dtype)


def _matmul_res_kernel(x_ref, w_ref, r_ref, o_ref, wb_ref):
    @pl.when(pl.program_id(1) == 0)
    def _():
        wb_ref[...] = w_ref[...].astype(BF16)

    o_ref[...] = r_ref[...] + jnp.dot(x_ref[...], wb_ref[...], preferred_element_type=F32)


def _matmul(x, w3, layer, col0, width, out_dtype, residual=None, silu=False, tm=1088, tn=512):
    assert not (silu and residual is not None)
    m, k = x.shape
    cb0 = col0 // tn
    grid = (width // tn, m // tm)
    in_specs = [pl.BlockSpec((tm, k), lambda j, i: (i, 0)),
                pl.BlockSpec((None, k, tn), lambda j, i: (layer, 0, cb0 + j))]
    args = [x, w3]
    kern = functools.partial(_matmul_kernel, silu)
    if residual is not None:
        in_specs.append(pl.BlockSpec((tm, tn), lambda j, i: (i, j)))
        args.append(residual)
        kern = _matmul_res_kernel
    vmem = 2 * (tm * k * 2 + k * tn * 4 + 2 * tm * tn * 4) + k * tn * 2 + (4 << 20)
    return pl.pallas_call(
        kern,
        grid=grid,
        in_specs=in_specs,
        out_specs=pl.BlockSpec((tm, tn), lambda j, i: (i, j)),
        out_shape=jax.ShapeDtypeStruct((m, width), out_dtype),
        scratch_shapes=[pltpu.VMEM((k, tn), BF16)],
        compiler_params=_cparams(("parallel", "arbitrary"), vmem),
    )(*args)


def _matmul_final_kernel(x_ref, w_ref, r_ref, o_ref, wb_ref):
    @pl.when((pl.program_id(1) == 0) & (pl.program_id(2) == 0))
    def _():
        wb_ref[...] = w_ref[...].astype(BF16)

    o_ref[...] = r_ref[...] + jnp.dot(x_ref[...], wb_ref[...], preferred_element_type=F32)


def _matmul_final(y3, w3, layer, h3, tm=1024, tn=512):
    batch, _, k = y3.shape
    d = h3.shape[-1]
    vmem = 2 * (tm * k * 2 + k * tn * 4 + 2 * tm * tn * 4) + k * tn * 2 + (4 << 20)
    return pl.pallas_call(
        _matmul_final_kernel,
        grid=(d // tn, batch, SEQ_REAL // tm),
        in_specs=[pl.BlockSpec((None, tm, k), lambda j, b, i: (b, i, 0)),
                  pl.BlockSpec((None, k, tn), lambda j, b, i: (layer, 0, j)),
                  pl.BlockSpec((None, tm, tn), lambda j, b, i: (b, i, j))],
        out_specs=pl.BlockSpec((None, tm, tn), lambda j, b, i: (b, i, j)),
        out_shape=jax.ShapeDtypeStruct((batch, SEQ_REAL, d), F32),
        scratch_shapes=[pltpu.VMEM((k, tn), BF16)],
        compiler_params=_cparams(("parallel", "arbitrary", "arbitrary"), vmem),
    )(y3, w3, h3)


def _norm_rope(x, gain, cos, sin_signed, half_rot):
    ms = jnp.mean(x * x, axis=-1, keepdims=True)
    xn = x * lax.rsqrt(ms + EPS) * gain
    lane = lax.broadcasted_iota(jnp.int32, (1, LANE), 1)
    partner_up = (lane & half_rot) == 0
    partner = jnp.where(partner_up,
                        pltpu.roll(xn, LANE - half_rot, 1),
                        pltpu.roll(xn, half_rot, 1))
    return xn * cos + partner * sin_signed


def _softmax_pv(q_bf, k_bf, v_bf, bias_last, denom_from_ones):
    s = lax.dot_general(q_bf, k_bf, (((1,), (1,)), ((), ())), preferred_element_type=F32)
    n_main = L_PAD - LANE
    s_main = s[:, :n_main]
    s_last = s[:, n_main:] + bias_last
    m = jnp.maximum(jnp.max(s_main, axis=-1, keepdims=True),
                    jnp.max(s_last, axis=-1, keepdims=True))
    p_main = jnp.exp2(s_main - m)
    p_last = jnp.exp2(s_last - m)
    p = jnp.concatenate([p_main.astype(BF16), p_last.astype(BF16)], axis=1)
    o = jnp.dot(p, v_bf, preferred_element_type=F32)
    if denom_from_ones:
        return o[:, :HEAD_DIM] * (1.0 / o[:, HEAD_DIM:HEAD_DIM + 1])
    l = jnp.sum(p_main, axis=-1, keepdims=True) + jnp.sum(p_last, axis=-1, keepdims=True)
    return o * (1.0 / l)


def _attn_a_kernel(sub, q_ref, k_ref, v_ref, cq_ref, sq_ref, ck_ref, sk_ref, gq_ref, gk_ref,
                   bias_ref, o_ref, kh_ref, vx_ref, qh_ref):
    @pl.when((pl.program_id(2) == 0) & (pl.program_id(3) == 0))
    def _():
        kh_ref[...] = _norm_rope(k_ref[...].astype(F32), gk_ref[...], ck_ref[...], sk_ref[...],
                                 HEAD_DIM // 4).astype(BF16)
        vx_ref[:, :HEAD_DIM] = v_ref[...]
        vx_ref[:, HEAD_DIM:] = jnp.ones((L_PAD, HEAD_DIM), BF16)

    q = _norm_rope(q_ref[...].astype(F32), gq_ref[...], cq_ref[...], sq_ref[...], HEAD_DIM // 4)
    qh_ref[...] = (q * Q_SCALE).astype(BF16)
    bias_last = bias_ref[:, L_PAD - LANE:]
    for i in range(q_ref.shape[0] // sub):
        rows = slice(i * sub, (i + 1) * sub)
        o_ref[rows, :] = _softmax_pv(qh_ref[rows, :], kh_ref[...], vx_ref[...], bias_last,
                                     True).astype(o_ref.dtype)


def _attn_a(proj, cos_t, sin_t, gq, gk, bias, batch, tq=L_PAD, sub=272):
    m = proj.shape[0]
    nq = L_PAD // tq
    qblk = lambda b, kv, g, qi: (b * nq + qi, COL_AQ // HEAD_DIM + kv * A_GROUPS + g)
    tabq = lambda b, kv, g, qi: (qi, 0)
    full = lambda b, kv, g, qi: (0, 0)
    return pl.pallas_call(
        functools.partial(_attn_a_kernel, sub),
        grid=(batch, A_KV_HEADS, A_GROUPS, nq),
        in_specs=[
            pl.BlockSpec((tq, HEAD_DIM), qblk),
            pl.BlockSpec((L_PAD, HEAD_DIM), lambda b, kv, g, qi: (b, COL_AK // HEAD_DIM + kv)),
            pl.BlockSpec((L_PAD, HEAD_DIM), lambda b, kv, g, qi: (b, COL_AV // HEAD_DIM + kv)),
            pl.BlockSpec((tq, HEAD_DIM), tabq),
            pl.BlockSpec((tq, HEAD_DIM), tabq),
            pl.BlockSpec((L_PAD, HEAD_DIM), full),
            pl.BlockSpec((L_PAD, HEAD_DIM), full),
            pl.BlockSpec((1, HEAD_DIM), full),
            pl.BlockSpec((1, HEAD_DIM), full),
            pl.BlockSpec((1, L_PAD), full),
        ],
        out_specs=pl.BlockSpec((tq, HEAD_DIM), lambda b, kv, g, qi: (b * nq + qi, kv * A_GROUPS + g)),
        out_shape=jax.ShapeDtypeStruct((m, BRANCH_W), BF16),
        scratch_shapes=[pltpu.VMEM((L_PAD, HEAD_DIM), BF16),
                        pltpu.VMEM((L_PAD, 2 * HEAD_DIM), BF16),
                        pltpu.VMEM((tq, HEAD_DIM), BF16)],
        compiler_params=_cparams(("parallel", "parallel", "arbitrary", "arbitrary"), 48 << 20),
    )(proj, proj, proj, cos_t, sin_t, cos_t, sin_t, gq.reshape(1, -1), gk.reshape(1, -1), bias)


def _attn_d_kernel(lam_init, sub, q_ref, k_ref, v_ref, cq_ref, sq_ref, ck_ref, sk_ref, gq_ref,
                   gk_ref, lam_ref, gout_ref, bias_ref, o_ref, kh_ref, qh_ref):
    halves = (slice(0, HEAD_DIM), slice(HEAD_DIM, 2 * HEAD_DIM))

    @pl.when(pl.program_id(2) == 0)
    def _():
        for sl in halves:
            kh_ref[:, sl] = _norm_rope(k_ref[:, sl].astype(F32), gk_ref[...], ck_ref[...],
                                       sk_ref[...], ROPE_DIMS // 2).astype(BF16)

    for sl in halves:
        q = _norm_rope(q_ref[:, sl].astype(F32), gq_ref[...], cq_ref[...], sq_ref[...],
                       ROPE_DIMS // 2)
        qh_ref[:, sl] = (q * Q_SCALE).astype(BF16)
    lf = lam_ref[...]
    lam = (jnp.exp(jnp.sum(lf[0:1] * lf[1:2], axis=-1, keepdims=True))
           - jnp.exp(jnp.sum(lf[2:3] * lf[3:4], axis=-1, keepdims=True)) + lam_init)
    bias_last = bias_ref[:, L_PAD - LANE:]
    for i in range(q_ref.shape[0] // sub):
        rows = slice(i * sub, (i + 1) * sub)
        o1, o2 = [_softmax_pv(qh_ref[rows, sl], kh_ref[:, sl], v_ref[...], bias_last, False)
                  for sl in halves]
        o = o1 - lam * o2
        ms = jnp.mean(o * o, axis=-1, keepdims=True)
        o = o * lax.rsqrt(ms + EPS) * gout_ref[...] * (1.0 - lam_init)
        o_ref[rows, :] = o.astype(o_ref.dtype)


def _attn_d(proj, cos_t, sin_t, gq, gk, lam_vec, gout, bias, lam_init, batch, tq=1088, sub=272):
    m = proj.shape[0]
    nq = L_PAD // tq
    w = 2 * HEAD_DIM
    tabq = lambda b, h, qi: (qi, 0)
    full = lambda b, h, qi: (0, 0)
    return pl.pallas_call(
        functools.partial(_attn_d_kernel, lam_init, sub),
        grid=(batch, D_HEADS, nq),
        in_specs=[
            pl.BlockSpec((tq, w), lambda b, h, qi: (b * nq + qi, COL_DQ // w + h)),
            pl.BlockSpec((L_PAD, w), lambda b, h, qi: (b, COL_DK // w + h)),
            pl.BlockSpec((L_PAD, w), lambda b, h, qi: (b, COL_DV // w + h)),
            pl.BlockSpec((tq, HEAD_DIM), tabq),
            pl.BlockSpec((tq, HEAD_DIM), tabq),
            pl.BlockSpec((L_PAD, HEAD_DIM), full),
            pl.BlockSpec((L_PAD, HEAD_DIM), full),
            pl.BlockSpec((1, HEAD_DIM), full),
            pl.BlockSpec((1, HEAD_DIM), full),
            pl.BlockSpec((4, HEAD_DIM), full),
            pl.BlockSpec((1, w), full),
            pl.BlockSpec((1, L_PAD), full),
        ],
        out_specs=pl.BlockSpec((tq, w), lambda b, h, qi: (b * nq + qi, h)),
        out_shape=jax.ShapeDtypeStruct((m, BRANCH_W), BF16),
        scratch_shapes=[pltpu.VMEM((L_PAD, w), BF16), pltpu.VMEM((tq, w), BF16)],
        compiler_params=_cparams(("parallel", "parallel", "arbitrary"), 48 << 20),
    )(proj, proj, proj, cos_t, sin_t, cos_t, sin_t, gq.reshape(1, -1), gk.reshape(1, -1),
      lam_vec, gout.reshape(1, -1), bias)


def _hyena_filter_kernel(z_ref, w1_ref, b1_ref, w2_ref, b2_ref, fr_ref, w3f_ref, w3b_ref,
                         dec_ref, o_ref, hid_ref):
    hp = lax.Precision.HIGHEST

    @pl.when((pl.program_id(0) == 0) & (pl.program_id(1) == 0))
    def _():
        fr = fr_ref[...]
        h1 = jnp.sin(fr * (jnp.dot(z_ref[...], w1_ref[...], precision=hp,
                                   preferred_element_type=F32) + b1_ref[...]))
        hid_ref[...] = jnp.sin(fr * (jnp.dot(h1, w2_ref[...], precision=hp,
                                             preferred_element_type=F32) + b2_ref[...]))

    hid = hid_ref[...]
    dec = dec_ref[...]
    lag = lax.broadcasted_iota(jnp.int32, (L_PAD, 1), 0)
    hf = jnp.dot(hid, w3f_ref[...], precision=hp, preferred_element_type=F32) * dec
    hb = jnp.dot(hid, w3b_ref[...], precision=hp, preferred_element_type=F32) * dec
    hf = jnp.where(lag < L_TRUE, hf, 0.0)
    hb = jnp.where((lag >= 1) & (lag < L_TRUE), hb, 0.0)
    o_ref[0] = (hf + hb).astype(o_ref.dtype)
    o_ref[1] = (hf - hb).astype(o_ref.dtype)


def _hyena_filter(zfeat, w1, b1, w2, b2, fr, w3, decay, tc=512):
    nct = BRANCH_W // tc
    full = lambda o, c: (0, 0)
    return pl.pallas_call(
        _hyena_filter_kernel,
        grid=(2, nct),
        in_specs=[
            pl.BlockSpec((L_PAD, LANE), full),
            pl.BlockSpec((LANE, LANE), full),
            pl.BlockSpec((1, LANE), full),
            pl.BlockSpec((LANE, LANE), full),
            pl.BlockSpec((1, LANE), full),
            pl.BlockSpec((1, LANE), full),
            pl.BlockSpec((LANE, tc), lambda o, c: (0, (o * 2) * nct + c)),
            pl.BlockSpec((LANE, tc), lambda o, c: (0, (o * 2 + 1) * nct + c)),
            pl.BlockSpec((L_PAD, tc), lambda o, c: (0, c)),
        ],
        out_specs=pl.BlockSpec((2, L_PAD, tc), lambda o, c: (0, 0, o * nct + c)),
        out_shape=jax.ShapeDtypeStruct((2, L_PAD, 2 * BRANCH_W), BF16),
        scratch_shapes=[pltpu.VMEM((L_PAD, LANE), F32)],
        compiler_params=_cparams(("arbitrary", "arbitrary"), 48 << 20),
    )(zfeat, w1, b1, w2, b2, fr, w3, w3, decay)


def _dft_tables_kernel(bc_ref, bs_ref, btc_ref, bts_ref, sc_ref, ss_ref, sct_ref, sst_ref,
                       f_ref, g_ref):
    sc, ss = sc_ref[...], ss_ref[...]
    bc, bs = bc_ref[...], bs_ref[...]
    f_ref[:FREQ_CHUNK, :] = (sc * bc - ss * bs).astype(f_ref.dtype)
    f_ref[FREQ_CHUNK:, :] = (ss * bc + sc * bs).astype(f_ref.dtype)
    sct, sst = sct_ref[...], sst_ref[...]
    btc, bts = btc_ref[...], bts_ref[...]
    g_ref[:, :FREQ_CHUNK] = (sct * btc - sst * bts).astype(g_ref.dtype)
    g_ref[:, FREQ_CHUNK:] = (sst * btc + sct * bts).astype(g_ref.dtype)


def _dft_tables():
    def trig(k, t):
        ang = ((k * t) % DFT_N).astype(F32) * (2.0 * math.pi / DFT_N)
        return jnp.cos(ang), jnp.sin(ang)

    t = jnp.arange(L_PAD, dtype=jnp.int32)
    fine = jnp.arange(16, dtype=jnp.int32)
    fc, fs = trig(fine[:, None], t[None, :])
    cc, cs = trig((fine * 16)[:, None], t[None, :])
    assert FREQ_CHUNK == 16 * 16
    bc = (cc[:, None] * fc[None] - cs[:, None] * fs[None]).reshape(FREQ_CHUNK, L_PAD)
    bs = (cs[:, None] * fc[None] + cc[:, None] * fs[None]).reshape(FREQ_CHUNK, L_PAD)
    live = (t < L_TRUE).astype(F32)[None, :]
    sc, ss = trig((jnp.arange(N_CHUNK, dtype=jnp.int32) * FREQ_CHUNK)[:, None], t[None, :])
    sc, ss = sc * live, ss * live
    full2 = lambda j: (0, 0)
    step_row = pl.BlockSpec((None, 1, L_PAD), lambda j: (j, 0, 0))
    step_col = pl.BlockSpec((None, L_PAD, 1), lambda j: (j, 0, 0))
    return pl.pallas_call(
        _dft_tables_kernel,
        grid=(N_CHUNK,),
        in_specs=[pl.BlockSpec((FREQ_CHUNK, L_PAD), full2), pl.BlockSpec((FREQ_CHUNK, L_PAD), full2),
                  pl.BlockSpec((L_PAD, FREQ_CHUNK), full2), pl.BlockSpec((L_PAD, FREQ_CHUNK), full2),
                  step_row, step_row, step_col, step_col],
        out_specs=[pl.BlockSpec((None, 2 * FREQ_CHUNK, L_PAD), lambda j: (j, 0, 0)),
                   pl.BlockSpec((None, L_PAD, 2 * FREQ_CHUNK), lambda j: (j, 0, 0))],
        out_shape=[jax.ShapeDtypeStruct((N_CHUNK, 2 * FREQ_CHUNK, L_PAD), BF16),
                   jax.ShapeDtypeStruct((N_CHUNK, L_PAD, 2 * FREQ_CHUNK), BF16)],
        compiler_params=_cparams(("parallel",), 48 << 20),
    )(bc, bs, bc.T, bs.T, sc[:, None, :], ss[:, None, :], sc[:, :, None], ss[:, :, None])


def _spectra_kernel(f_ref, h_ref, o_ref):
    k = pl.program_id(1) * FREQ_CHUNK + lax.broadcasted_iota(jnp.int32, (FREQ_CHUNK, 1), 0)
    wk = jnp.where(k == 0, 1.0 / DFT_N, 2.0 / DFT_N)
    o_ref[:FREQ_CHUNK, :] = jnp.dot(f_ref[:FREQ_CHUNK, :], h_ref[0],
                                    preferred_element_type=F32) * wk
    o_ref[FREQ_CHUNK:, :] = jnp.dot(f_ref[FREQ_CHUNK:, :], h_ref[1],
                                    preferred_element_type=F32) * wk


def _spectra(fmat, hsd, tn=1024):
    ncol = hsd.shape[-1]
    return pl.pallas_call(
        _spectra_kernel,
        grid=(ncol // tn, N_CHUNK),
        in_specs=[pl.BlockSpec((None, 2 * FREQ_CHUNK, L_PAD), lambda n, f: (f, 0, 0)),
                  pl.BlockSpec((2, L_PAD, tn), lambda n, f: (0, 0, n))],
        out_specs=pl.BlockSpec((None, 2 * FREQ_CHUNK, tn), lambda n, f: (f, 0, n)),
        out_shape=jax.ShapeDtypeStruct((N_CHUNK, 2 * FREQ_CHUNK, ncol), F32),
        compiler_params=_cparams(("parallel", "arbitrary"), 40 << 20),
    )(fmat, hsd)


def _short_conv3(x, w):
    return (pltpu.roll(x, 1, 0) * w[0:1] + x * w[1:2] + pltpu.roll(x, L_PAD - 1, 0) * w[2:3])


def _hyena_order_kernel(first, z_ref, g_ref, wz_ref, wg_ref, skip_ref, f_ref, gm_ref, pq_ref,
                        o_ref, zs_ref, zf_ref, acc_ref):
    fk = pl.program_id(2)

    @pl.when(fk == 0)
    def _():
        z = z_ref[...].astype(F32)
        if first:
            z = _short_conv3(_to_sequence_order(z), wz_ref[...])
        zf_ref[...] = z
        zs_ref[...] = z.astype(BF16)
        acc_ref[...] = jnp.zeros_like(acc_ref)

    ab = jnp.dot(f_ref[...], zs_ref[...], preferred_element_type=F32)
    a, b = ab[:FREQ_CHUNK], ab[FREQ_CHUNK:]
    p, q = pq_ref[:FREQ_CHUNK, :], pq_ref[FREQ_CHUNK:, :]
    uv = jnp.concatenate([(a * p - b * q).astype(BF16), (a * q + b * p).astype(BF16)], axis=0)
    acc_ref[...] += jnp.dot(gm_ref[...], uv, preferred_element_type=F32)

    @pl.when(fk == pl.num_programs(2) - 1)
    def _():
        gate = _short_conv3(_to_sequence_order(g_ref[...].astype(F32)), wg_ref[...])
        y = gate * (acc_ref[...] + skip_ref[...] * zf_ref[...])
        if first:
            o_ref[...] = y
        else:
            _store_from_sequence_order(o_ref, y)


def _hyena_order(order, zin, zin_col0, proj, conv_w, skip, fmat, gmat, pq, batch, tc=512):
    first = order == 0
    m = proj.shape[0]
    nct = BRANCH_W // tc
    zcb = zin_col0 // tc
    gcb = (COL_B + (order + 1) * BRANCH_W) // tc
    in_specs = [
        pl.BlockSpec((L_PAD, tc), lambda b, c, f: (b, zcb + c)),
        pl.BlockSpec((L_PAD, tc), lambda b, c, f: (b, gcb + c)),
        pl.BlockSpec((3, tc), lambda b, c, f: (0, c)),
        pl.BlockSpec((3, tc), lambda b, c, f: (0, (order + 1) * nct + c)),
        pl.BlockSpec((1, tc), lambda b, c, f: (0, c)),
        pl.BlockSpec((None, 2 * FREQ_CHUNK, L_PAD), lambda b, c, f: (f, 0, 0)),
        pl.BlockSpec((None, L_PAD, 2 * FREQ_CHUNK), lambda b, c, f: (f, 0, 0)),
        pl.BlockSpec((None, 2 * FREQ_CHUNK, tc), lambda b, c, f: (f, 0, order * nct + c)),
    ]
    return pl.pallas_call(
        functools.partial(_hyena_order_kernel, first),
        grid=(batch, nct, N_CHUNK),
        in_specs=in_specs,
        out_specs=pl.BlockSpec((L_PAD, tc), lambda b, c, f: (b, c)),
        out_shape=jax.ShapeDtypeStruct((m, BRANCH_W), F32),
        scratch_shapes=[pltpu.VMEM((L_PAD, tc), BF16),
                        pltpu.VMEM((L_PAD, tc), F32),
                        pltpu.VMEM((L_PAD, tc), F32)],
        compiler_params=_cparams(("parallel", "parallel", "arbitrary"), VMEM_CAP),
    )(zin, proj, conv_w, conv_w, skip[order].reshape(1, BRANCH_W), fmat, gmat, pq)


def _pool_kernel(p0_ref, p1_ref, p2_ref, p3_ref, gw_ref, scale_ref, o_ref):
    t = lax.broadcasted_iota(jnp.int32, (L_PAD, 1), 0)
    for g, (w, p_ref) in enumerate(zip(POOL_WINDOWS, (p0_ref, p1_ref, p2_ref, p3_ref))):
        p = _to_sequence_order(p_ref[...].astype(F32))
        s = p
        span = 1
        while span < w:
            s = s + pltpu.roll(s, L_PAD - span, 0)
            span *= 2
        back = (w - 1) // 2
        if back:
            s = pltpu.roll(s, back, 0)
        lo = jnp.clip(t - back, 0, L_TRUE)
        hi = jnp.clip(t + w // 2 + 1, 0, L_TRUE)
        cnt = jnp.maximum(hi - lo, 1).astype(F32)
        d = (s / cnt - p).astype(BF16)
        y = jnp.dot(d, gw_ref[g].astype(BF16), preferred_element_type=F32)
        sl = slice(g * POOL_GROUP_W, (g + 1) * POOL_GROUP_W)
        _store_from_sequence_order(o_ref, y * scale_ref[:, sl], sl)


def _pool(proj, group_w, scale, batch):
    m = proj.shape[0]
    cb = COL_C // POOL_GROUP_W
    grp = lambda g: pl.BlockSpec((L_PAD, POOL_GROUP_W), lambda b: (b, cb + g))
    return pl.pallas_call(
        _pool_kernel,
        grid=(batch,),
        in_specs=[grp(0), grp(1), grp(2), grp(3),
                  pl.BlockSpec((4, POOL_GROUP_W, POOL_GROUP_W), lambda b: (0, 0, 0)),
                  pl.BlockSpec((1, BRANCH_W), lambda b: (0, 0))],
        out_specs=pl.BlockSpec((L_PAD, BRANCH_W), lambda b: (b, 0)),
        out_shape=jax.ShapeDtypeStruct((m, BRANCH_W), BF16),
        compiler_params=_cparams(("parallel",), 48 << 20),
    )(proj, proj, proj, proj, group_w, scale.reshape(1, -1))


GATE_ROWS = 16


def _gate_kernel(tm, oa_ref, zb_ref, yc_ref, od_ref, gate_ref, na_ref, nb_ref, y_ref):
    branches = ((oa_ref, na_ref), (zb_ref, nb_ref), (yc_ref, None), (od_ref, None))

    def row_group(r, carry):
        row0 = pl.multiple_of(r * GATE_ROWS, GATE_ROWS)
        rows = pl.ds(row0, GATE_ROWS)
        pos = (pl.program_id(1) * tm + row0
               + lax.broadcasted_iota(jnp.int32, (GATE_ROWS, 1), 0))
        live = pos < L_TRUE
        for i, (x_ref, norm_ref) in enumerate(branches):
            sl = slice(i * BRANCH_W, (i + 1) * BRANCH_W)
            y = x_ref[rows, :].astype(F32)
            if norm_ref is not None:
                ms = jnp.mean(y * y, axis=-1, keepdims=True)
                y = y * lax.rsqrt(ms + EPS) * norm_ref[...]
            y = y * gate_ref[rows, sl].astype(F32)
            y_ref[rows, sl] = jnp.where(live, y, 0.0).astype(y_ref.dtype)
        return carry

    lax.fori_loop(0, tm // GATE_ROWS, row_group, 0)


def _gate(oa, zb, yc, od, gate, na, nb, batch, tm=272):
    m = oa.shape[0]
    nt = L_PAD // tm
    row = lambda b, t: (b * nt + t, 0)
    full = lambda b, t: (0, 0)
    br = pl.BlockSpec((tm, BRANCH_W), row)
    return pl.pallas_call(
        functools.partial(_gate_kernel, tm),
        grid=(batch, nt),
        in_specs=[br, br, br, br, pl.BlockSpec((tm, D_MODEL), row),
                  pl.BlockSpec((1, BRANCH_W), full), pl.BlockSpec((1, BRANCH_W), full)],
        out_specs=pl.BlockSpec((tm, D_MODEL), row),
        out_shape=jax.ShapeDtypeStruct((m, D_MODEL), BF16),
        compiler_params=_cparams(("parallel", "parallel"), 32 << 20),
    )(oa, zb, yc, od, gate, na.reshape(1, -1), nb.reshape(1, -1))


def _inv_freq(dim, theta):
    return theta ** (-jnp.arange(0, dim, 2, dtype=F32) / dim)


def _pad_rows(t):
    return jnp.pad(t, ((0, L_PAD - t.shape[0]), (0, 0)))


def _rope_tables():
    rows = SEQ_REAL // GRID_W
    zeros_meta = jnp.zeros((N_META,), F32)
    row = jnp.concatenate([zeros_meta, jnp.repeat(jnp.arange(rows, dtype=F32), GRID_W)])
    col = jnp.concatenate([zeros_meta, jnp.tile(jnp.arange(GRID_W, dtype=F32), rows)])
    axf = _inv_freq(HEAD_DIM // 2, AXIAL_THETA)
    ar = row[:, None] * axf[None, :]
    ac = col[:, None] * axf[None, :]
    cos_a = jnp.concatenate([jnp.cos(ar), jnp.cos(ar), jnp.cos(ac), jnp.cos(ac)], axis=-1)
    sin_a = jnp.concatenate([-jnp.sin(ar), jnp.sin(ar), -jnp.sin(ac), jnp.sin(ac)], axis=-1)
    a1 = jnp.arange(L_TRUE, dtype=F32)[:, None] * _inv_freq(ROPE_DIMS, ROPE_THETA)[None, :]
    rest = HEAD_DIM - ROPE_DIMS
    cos_d = jnp.concatenate([jnp.cos(a1), jnp.cos(a1), jnp.ones((L_TRUE, rest), F32)], axis=-1)
    sin_d = jnp.concatenate([-jnp.sin(a1), jnp.sin(a1), jnp.zeros((L_TRUE, rest), F32)], axis=-1)
    def hbm_rows(tab):
        return _pad_rows(jnp.concatenate([tab[N_META:], tab[:N_META]], axis=0))

    return hbm_rows(cos_a), hbm_rows(sin_a), hbm_rows(cos_d), hbm_rows(sin_d)


def _hyena_tables():
    t = jnp.linspace(0.0, 1.0, L_TRUE, dtype=F32)[:, None]
    bands = (HYENA_EMB_DIM - 1) // 2
    fb = jnp.linspace(1e-4, bands - 1, bands, dtype=F32)[None, :]
    wpos = 2.0 * math.pi * jnp.arange(L_TRUE, dtype=F32)[:, None] / L_TRUE
    z = jnp.concatenate([t, jnp.cos(fb * wpos), -jnp.sin(fb * wpos)], axis=-1)
    z = jnp.pad(z, ((0, L_PAD - L_TRUE), (0, LANE - HYENA_EMB_DIM)))
    deltas = jnp.abs(jnp.linspace(math.log(HYENA_TARGET) / HYENA_SLOW,
                                  math.log(HYENA_TARGET) / HYENA_FAST, BRANCH_W, dtype=F32))
    decay = _pad_rows(jnp.exp(-t * deltas) + HYENA_SHIFT)
    return z, decay


def _pad2(a, rows, cols):
    return jnp.pad(a, ((0, rows - a.shape[0]), (0, cols - a.shape[1])))


def kernel(x, meta_tokens, norm_w, w_in, w_out, a_q_norm, a_k_norm, a_out_norm, b_short_conv,
           b_filt_w1, b_filt_b1, b_filt_w2, b_filt_b2, b_filt_w3, b_sin_freq, b_skip, b_out_norm,
           c_group_w, c_scale, d_q_norm, d_k_norm, d_lambda, d_out_norm):
    batch = x.shape[0]
    depth = norm_w.shape[0]
    m = batch * L_PAD
    tail = _pad2(meta_tokens.astype(x.dtype), L_PAD - SEQ_REAL, D_MODEL)

    cos_a, sin_a, cos_d, sin_d = _rope_tables()
    fmat, gmat = _dft_tables()
    zfeat, decay = _hyena_tables()
    bias = jnp.where(jnp.arange(L_PAD) < L_TRUE, 0.0, NEG_BIG).astype(F32).reshape(1, L_PAD)

    for l in range(depth):
        lam_init = 0.8 - 0.6 * math.exp(-0.3 * l)
        if l == 0:
            h, u = _embed_rmsnorm(x, tail, norm_w[0])
            h, u = h.reshape(m, D_MODEL), u.reshape(m, D_MODEL)
        else:
            u = _rmsnorm(h, norm_w[l])
        proj = _matmul(u, w_in, l, 0, MAIN_W, BF16)
        gate = _matmul(u, w_in, l, COL_GATE, D_MODEL, BF16, silu=True)

        oa = _attn_a(proj, cos_a, sin_a, a_q_norm[l], a_k_norm[l], bias, batch)
        od = _attn_d(proj, cos_d, sin_d, d_q_norm[l], d_k_norm[l], d_lambda[l], d_out_norm[l],
                     bias, lam_init, batch)

        hsd = _hyena_filter(
            zfeat,
            _pad2(b_filt_w1[l], LANE, LANE), _pad2(b_filt_b1[l][None], 1, LANE),
            _pad2(b_filt_w2[l], LANE, LANE), _pad2(b_filt_b2[l][None], 1, LANE),
            _pad2(b_sin_freq[l][None], 1, LANE), _pad2(b_filt_w3[l], LANE, 4 * BRANCH_W), decay)
        pq = _spectra(fmat, hsd)
        z1 = _hyena_order(0, proj, COL_B, proj, b_short_conv[l], b_skip[l], fmat, gmat, pq, batch)
        z2 = _hyena_order(1, z1, 0, proj, b_short_conv[l], b_skip[l], fmat, gmat, pq, batch)

        yc = _pool(proj, c_group_w[l], c_scale[l], batch)
        y = _gate(oa, z2, yc, od, gate, a_out_norm[l], b_out_norm[l], batch)
        if l < depth - 1:
            h = _matmul(y, w_out, l, 0, D_MODEL, F32, residual=h)
        else:
            out = _matmul_final(y.reshape(batch, L_PAD, D_MODEL), w_out, l,
                                h.reshape(batch, L_PAD, D_MODEL))
    return out
```

```python
import functools
import math

import jax
import jax.numpy as jnp
from jax import lax
from jax.experimental import pallas as pl
from jax.experimental.pallas import tpu as pltpu

F32 = jnp.float32
BF16 = jnp.bfloat16

D_MODEL = 4096
N_META = 16
GRID_W = 64
EPS = 1e-6
BRANCH_W = 1024
HEAD_DIM = 128
A_HEADS = 8
A_KV_HEADS = 2
A_GROUPS = A_HEADS // A_KV_HEADS
AXIAL_THETA = 10000.0
HYENA_EMB_DIM = 33
HYENA_FFN = 64
HYENA_FAST = 0.3
HYENA_SLOW = 1.5
HYENA_TARGET = 1e-2
HYENA_SHIFT = 0.05
POOL_WINDOWS = (2, 4, 8, 16)
POOL_GROUP_W = 256
D_HEADS = 4
D_HEAD_V = 256
ROPE_THETA = 500000.0
ROPE_DIMS = 32

COL_AQ = 0
COL_AK = 1024
COL_AV = 1280
COL_B = 1536
COL_C = 4608
COL_DQ = 5632
COL_DK = 6656
COL_DV = 7680
COL_GATE = 8704
MAIN_W = COL_GATE
IN_COLS = 12800

LANE = 128
SEQ_REAL = 2048
L_TRUE = N_META + SEQ_REAL
L_PAD = 2176
DFT_N = 4607
N_FREQ = 2304
FREQ_CHUNK = 384
N_CHUNK = N_FREQ // FREQ_CHUNK
NEG_BIG = -1e30
LOG2E = 1.4426950408889634
Q_SCALE = (HEAD_DIM ** -0.5) * LOG2E
assert L_TRUE > L_PAD - LANE
VMEM_CAP = 56 * 1024 * 1024


def _cparams(semantics, vmem_bytes):
    return pltpu.CompilerParams(dimension_semantics=semantics,
                                vmem_limit_bytes=min(int(vmem_bytes), VMEM_CAP))


def _to_sequence_order(x):
    return jnp.concatenate([x[SEQ_REAL:L_TRUE], x[:SEQ_REAL], x[L_TRUE:]], axis=0)


def _store_from_sequence_order(o_ref, y, cols=slice(None)):
    o_ref[:SEQ_REAL, cols] = y[N_META:L_TRUE].astype(o_ref.dtype)
    o_ref[SEQ_REAL:L_TRUE, cols] = y[:N_META].astype(o_ref.dtype)
    o_ref[L_TRUE:, cols] = y[L_TRUE:].astype(o_ref.dtype)


def _rmsnorm_kernel(h_ref, w_ref, o_ref):
    x = h_ref[...]
    ms = jnp.mean(x * x, axis=-1, keepdims=True)
    o_ref[...] = (x * lax.rsqrt(ms + EPS) * w_ref[...]).astype(o_ref.dtype)


def _embed_rmsnorm_kernel(n_real_tiles, x_ref, tail_ref, w_ref, h_ref, u_ref):
    def emit(x):
        h_ref[...] = x
        ms = jnp.mean(x * x, axis=-1, keepdims=True)
        u_ref[...] = (x * lax.rsqrt(ms + EPS) * w_ref[...]).astype(u_ref.dtype)

    @pl.when(pl.program_id(1) < n_real_tiles)
    def _():
        emit(x_ref[...])

    @pl.when(pl.program_id(1) == n_real_tiles)
    def _():
        emit(tail_ref[...])


def _embed_rmsnorm(x, tail, w, tm=LANE):
    batch, n_real, d = x.shape
    n_real_tiles = n_real // tm
    row = lambda b, i: (b, i, 0)
    return pl.pallas_call(
        functools.partial(_embed_rmsnorm_kernel, n_real_tiles),
        grid=(batch, L_PAD // tm),
        in_specs=[pl.BlockSpec((None, tm, d), lambda b, i: (b, jnp.minimum(i, n_real_tiles - 1), 0)),
                  pl.BlockSpec((tm, d), lambda b, i: (0, 0)),
                  pl.BlockSpec((1, d), lambda b, i: (0, 0))],
        out_specs=[pl.BlockSpec((None, tm, d), row), pl.BlockSpec((None, tm, d), row)],
        out_shape=[jax.ShapeDtypeStruct((batch, L_PAD, d), F32),
                   jax.ShapeDtypeStruct((batch, L_PAD, d), BF16)],
        compiler_params=_cparams(("parallel", "arbitrary"), 32 << 20),
    )(x, tail, w.reshape(1, d))


def _rmsnorm(h, w, tm=272):
    m, d = h.shape
    return pl.pallas_call(
        _rmsnorm_kernel,
        grid=(m // tm,),
        in_specs=[pl.BlockSpec((tm, d), lambda i: (i, 0)),
                  pl.BlockSpec((1, d), lambda i: (0, 0))],
        out_specs=pl.BlockSpec((tm, d), lambda i: (i, 0)),
        out_shape=jax.ShapeDtypeStruct((m, d), BF16),
        compiler_params=_cparams(("parallel",), 32 << 20),
    )(h, w.reshape(1, d))


def _matmul_kernel(x_ref, w_ref, o_ref, wb_ref):
    @pl.when(pl.program_id(1) == 0)
    def _():
        wb_ref[...] = w_ref[...].astype(BF16)

    o_ref[...] = jnp.dot(x_ref[...], wb_ref[...],
                         preferred_element_type=F32).astype(o_ref.dtype)


def _matmul_res_kernel(x_ref, w_ref, r_ref, o_ref, wb_ref):
    @pl.when(pl.program_id(1) == 0)
    def _():
        wb_ref[...] = w_ref[...].astype(BF16)

    o_ref[...] = r_ref[...] + jnp.dot(x_ref[...], wb_ref[...], preferred_element_type=F32)


def _matmul(x, w3, layer, col0, width, out_dtype, residual=None, tm=1088, tn=512):
    m, k = x.shape
    cb0 = col0 // tn
    grid = (width // tn, m // tm)
    in_specs = [pl.BlockSpec((tm, k), lambda j, i: (i, 0)),
                pl.BlockSpec((None, k, tn), lambda j, i: (layer, 0, cb0 + j))]
    args = [x, w3]
    kern = _matmul_kernel
    if residual is not None:
        in_specs.append(pl.BlockSpec((tm, tn), lambda j, i: (i, j)))
        args.append(residual)
        kern = _matmul_res_kernel
    vmem = 2 * (tm * k * 2 + k * tn * 4 + 2 * tm * tn * 4) + k * tn * 2 + (4 << 20)
    return pl.pallas_call(
        kern,
        grid=grid,
        in_specs=in_specs,
        out_specs=pl.BlockSpec((tm, tn), lambda j, i: (i, j)),
        out_shape=jax.ShapeDtypeStruct((m, width), out_dtype),
        scratch_shapes=[pltpu.VMEM((k, tn), BF16)],
        compiler_params=_cparams(("parallel", "arbitrary"), vmem),
    )(*args)


def _matmul_final_kernel(x_ref, w_ref, r_ref, o_ref, wb_ref):
    @pl.when((pl.program_id(1) == 0) & (pl.program_id(2) == 0))
    def _():
        wb_ref[...] = w_ref[...].astype(BF16)

    o_ref[...] = r_ref[...] + jnp.dot(x_ref[...], wb_ref[...], preferred_element_type=F32)


def _matmul_final(y3, w3, layer, h3, tm=1024, tn=512):
    batch, _, k = y3.shape
    d = h3.shape[-1]
    vmem = 2 * (tm * k * 2 + k * tn * 4 + 2 * tm * tn * 4) + k * tn * 2 + (4 << 20)
    return pl.pallas_call(
        _matmul_final_kernel,
        grid=(d // tn, batch, SEQ_REAL // tm),
        in_specs=[pl.BlockSpec((None, tm, k), lambda j, b, i: (b, i, 0)),
                  pl.BlockSpec((None, k, tn), lambda j, b, i: (layer, 0, j)),
                  pl.BlockSpec((None, tm, tn), lambda j, b, i: (b, i, j))],
        out_specs=pl.BlockSpec((None, tm, tn), lambda j, b, i: (b, i, j)),
        out_shape=jax.ShapeDtypeStruct((batch, SEQ_REAL, d), F32),
        scratch_shapes=[pltpu.VMEM((k, tn), BF16)],
        compiler_params=_cparams(("parallel", "arbitrary", "arbitrary"), vmem),
    )(y3, w3, h3)


def _norm_rope(x, gain, cos, sin_signed, half_rot):
    ms = jnp.mean(x * x, axis=-1, keepdims=True)
    xn = x * lax.rsqrt(ms + EPS) * gain
    lane = lax.broadcasted_iota(jnp.int32, (1, LANE), 1)
    partner_up = (lane & half_rot) == 0
    partner = jnp.where(partner_up,
                        pltpu.roll(xn, LANE - half_rot, 1),
                        pltpu.roll(xn, half_rot, 1))
    return xn * cos + partner * sin_signed


def _softmax_pv(q_bf, k_bf, v_bf, bias_last, denom_from_ones):
    s = lax.dot_general(q_bf, k_bf, (((1,), (1,)), ((), ())), preferred_element_type=F32)
    n_main = L_PAD - LANE
    s_main = s[:, :n_main]
    s_last = s[:, n_main:] + bias_last
    m = jnp.maximum(jnp.max(s_main, axis=-1, keepdims=True),
                    jnp.max(s_last, axis=-1, keepdims=True))
    p_main = jnp.exp2(s_main - m)
    p_last = jnp.exp2(s_last - m)
    p = jnp.concatenate([p_main.astype(BF16), p_last.astype(BF16)], axis=1)
    o = jnp.dot(p, v_bf, preferred_element_type=F32)
    if denom_from_ones:
        return o[:, :HEAD_DIM] * (1.0 / o[:, HEAD_DIM:HEAD_DIM + 1])
    l = jnp.sum(p_main, axis=-1, keepdims=True) + jnp.sum(p_last, axis=-1, keepdims=True)
    return o * (1.0 / l)


def _attn_a_kernel(sub, q_ref, k_ref, v_ref, cq_ref, sq_ref, ck_ref, sk_ref, gq_ref, gk_ref,
                   bias_ref, o_ref, kh_ref, vx_ref, qh_ref):
    @pl.when((pl.program_id(2) == 0) & (pl.program_id(3) == 0))
    def _():
        kh_ref[...] = _norm_rope(k_ref[...].astype(F32), gk_ref[...], ck_ref[...], sk_ref[...],
                                 HEAD_DIM // 4).astype(BF16)
        vx_ref[:, :HEAD_DIM] = v_ref[...]
        vx_ref[:, HEAD_DIM:] = jnp.ones((L_PAD, HEAD_DIM), BF16)

    q = _norm_rope(q_ref[...].astype(F32), gq_ref[...], cq_ref[...], sq_ref[...], HEAD_DIM // 4)
    qh_ref[...] = (q * Q_SCALE).astype(BF16)
    bias_last = bias_ref[:, L_PAD - LANE:]
    for i in range(q_ref.shape[0] // sub):
        rows = slice(i * sub, (i + 1) * sub)
        o_ref[rows, :] = _softmax_pv(qh_ref[rows, :], kh_ref[...], vx_ref[...], bias_last,
                                     True).astype(o_ref.dtype)


def _attn_a(proj, cos_t, sin_t, gq, gk, bias, batch, tq=L_PAD, sub=272):
    m = proj.shape[0]
    nq = L_PAD // tq
    qblk = lambda b, kv, g, qi: (b * nq + qi, COL_AQ // HEAD_DIM + kv * A_GROUPS + g)
    tabq = lambda b, kv, g, qi: (qi, 0)
    full = lambda b, kv, g, qi: (0, 0)
    return pl.pallas_call(
        functools.partial(_attn_a_kernel, sub),
        grid=(batch, A_KV_HEADS, A_GROUPS, nq),
        in_specs=[
            pl.BlockSpec((tq, HEAD_DIM), qblk),
            pl.BlockSpec((L_PAD, HEAD_DIM), lambda b, kv, g, qi: (b, COL_AK // HEAD_DIM + kv)),
            pl.BlockSpec((L_PAD, HEAD_DIM), lambda b, kv, g, qi: (b, COL_AV // HEAD_DIM + kv)),
            pl.BlockSpec((tq, HEAD_DIM), tabq),
            pl.BlockSpec((tq, HEAD_DIM), tabq),
            pl.BlockSpec((L_PAD, HEAD_DIM), full),
            pl.BlockSpec((L_PAD, HEAD_DIM), full),
            pl.BlockSpec((1, HEAD_DIM), full),
            pl.BlockSpec((1, HEAD_DIM), full),
            pl.BlockSpec((1, L_PAD), full),
        ],
        out_specs=pl.BlockSpec((tq, HEAD_DIM), lambda b, kv, g, qi: (b * nq + qi, kv * A_GROUPS + g)),
        out_shape=jax.ShapeDtypeStruct((m, BRANCH_W), BF16),
        scratch_shapes=[pltpu.VMEM((L_PAD, HEAD_DIM), BF16),
                        pltpu.VMEM((L_PAD, 2 * HEAD_DIM), BF16),
                        pltpu.VMEM((tq, HEAD_DIM), BF16)],
        compiler_params=_cparams(("parallel", "parallel", "arbitrary", "arbitrary"), 48 << 20),
    )(proj, proj, proj, cos_t, sin_t, cos_t, sin_t, gq.reshape(1, -1), gk.reshape(1, -1), bias)


def _attn_d_kernel(lam_init, sub, q_ref, k_ref, v_ref, cq_ref, sq_ref, ck_ref, sk_ref, gq_ref,
                   gk_ref, lam_ref, gout_ref, bias_ref, o_ref, kh_ref, qh_ref):
    halves = (slice(0, HEAD_DIM), slice(HEAD_DIM, 2 * HEAD_DIM))

    @pl.when(pl.program_id(2) == 0)
    def _():
        for sl in halves:
            kh_ref[:, sl] = _norm_rope(k_ref[:, sl].astype(F32), gk_ref[...], ck_ref[...],
                                       sk_ref[...], ROPE_DIMS // 2).astype(BF16)

    for sl in halves:
        q = _norm_rope(q_ref[:, sl].astype(F32), gq_ref[...], cq_ref[...], sq_ref[...],
                       ROPE_DIMS // 2)
        qh_ref[:, sl] = (q * Q_SCALE).astype(BF16)
    lf = lam_ref[...]
    lam = (jnp.exp(jnp.sum(lf[0:1] * lf[1:2], axis=-1, keepdims=True))
           - jnp.exp(jnp.sum(lf[2:3] * lf[3:4], axis=-1, keepdims=True)) + lam_init)
    bias_last = bias_ref[:, L_PAD - LANE:]
    for i in range(q_ref.shape[0] // sub):
        rows = slice(i * sub, (i + 1) * sub)
        o1, o2 = [_softmax_pv(qh_ref[rows, sl], kh_ref[:, sl], v_ref[...], bias_last, False)
                  for sl in halves]
        o = o1 - lam * o2
        ms = jnp.mean(o * o, axis=-1, keepdims=True)
        o = o * lax.rsqrt(ms + EPS) * gout_ref[...] * (1.0 - lam_init)
        o_ref[rows, :] = o.astype(o_ref.dtype)


def _attn_d(proj, cos_t, sin_t, gq, gk, lam_vec, gout, bias, lam_init, batch, tq=1088, sub=272):
    m = proj.shape[0]
    nq = L_PAD // tq
    w = 2 * HEAD_DIM
    tabq = lambda b, h, qi: (qi, 0)
    full = lambda b, h, qi: (0, 0)
    return pl.pallas_call(
        functools.partial(_attn_d_kernel, lam_init, sub),
        grid=(batch, D_HEADS, nq),
        in_specs=[
            pl.BlockSpec((tq, w), lambda b, h, qi: (b * nq + qi, COL_DQ // w + h)),
            pl.BlockSpec((L_PAD, w), lambda b, h, qi: (b, COL_DK // w + h)),
            pl.BlockSpec((L_PAD, w), lambda b, h, qi: (b, COL_DV // w + h)),
            pl.BlockSpec((tq, HEAD_DIM), tabq),
            pl.BlockSpec((tq, HEAD_DIM), tabq),
            pl.BlockSpec((L_PAD, HEAD_DIM), full),
            pl.BlockSpec((L_PAD, HEAD_DIM), full),
            pl.BlockSpec((1, HEAD_DIM), full),
            pl.BlockSpec((1, HEAD_DIM), full),
            pl.BlockSpec((4, HEAD_DIM), full),
            pl.BlockSpec((1, w), full),
            pl.BlockSpec((1, L_PAD), full),
        ],
        out_specs=pl.BlockSpec((tq, w), lambda b, h, qi: (b * nq + qi, h)),
        out_shape=jax.ShapeDtypeStruct((m, BRANCH_W), BF16),
        scratch_shapes=[pltpu.VMEM((L_PAD, w), BF16), pltpu.VMEM((tq, w), BF16)],
        compiler_params=_cparams(("parallel", "parallel", "arbitrary"), 48 << 20),
    )(proj, proj, proj, cos_t, sin_t, cos_t, sin_t, gq.reshape(1, -1), gk.reshape(1, -1),
      lam_vec, gout.reshape(1, -1), bias)


def _hyena_filter_kernel(z_ref, w1_ref, b1_ref, w2_ref, b2_ref, fr_ref, w3f_ref, w3b_ref,
                         dec_ref, o_ref, hid_ref):
    hp = lax.Precision.HIGHEST

    @pl.when((pl.program_id(0) == 0) & (pl.program_id(1) == 0))
    def _():
        fr = fr_ref[...]
        h1 = jnp.sin(fr * (jnp.dot(z_ref[...], w1_ref[...], precision=hp,
                                   preferred_element_type=F32) + b1_ref[...]))
        hid_ref[...] = jnp.sin(fr * (jnp.dot(h1, w2_ref[...], precision=hp,
                                             preferred_element_type=F32) + b2_ref[...]))

    hid = hid_ref[...]
    dec = dec_ref[...]
    lag = lax.broadcasted_iota(jnp.int32, (L_PAD, 1), 0)
    hf = jnp.dot(hid, w3f_ref[...], precision=hp, preferred_element_type=F32) * dec
    hb = jnp.dot(hid, w3b_ref[...], precision=hp, preferred_element_type=F32) * dec
    hf = jnp.where(lag < L_TRUE, hf, 0.0)
    hb = jnp.where((lag >= 1) & (lag < L_TRUE), hb, 0.0)
    o_ref[0] = (hf + hb).astype(o_ref.dtype)
    o_ref[1] = (hf - hb).astype(o_ref.dtype)


def _hyena_filter(zfeat, w1, b1, w2, b2, fr, w3, decay, tc=512):
    nct = BRANCH_W // tc
    full = lambda o, c: (0, 0)
    return pl.pallas_call(
        _hyena_filter_kernel,
        grid=(2, nct),
        in_specs=[
            pl.BlockSpec((L_PAD, LANE), full),
            pl.BlockSpec((LANE, LANE), full),
            pl.BlockSpec((1, LANE), full),
            pl.BlockSpec((LANE, LANE), full),
            pl.BlockSpec((1, LANE), full),
            pl.BlockSpec((1, LANE), full),
            pl.BlockSpec((LANE, tc), lambda o, c: (0, (o * 2) * nct + c)),
            pl.BlockSpec((LANE, tc), lambda o, c: (0, (o * 2 + 1) * nct + c)),
            pl.BlockSpec((L_PAD, tc), lambda o, c: (0, c)),
        ],
        out_specs=pl.BlockSpec((2, L_PAD, tc), lambda o, c: (0, 0, o * nct + c)),
        out_shape=jax.ShapeDtypeStruct((2, L_PAD, 2 * BRANCH_W), BF16),
        scratch_shapes=[pltpu.VMEM((L_PAD, LANE), F32)],
        compiler_params=_cparams(("arbitrary", "arbitrary"), 48 << 20),
    )(zfeat, w1, b1, w2, b2, fr, w3, w3, decay)


def _dft_tables_kernel(bc_ref, bs_ref, btc_ref, bts_ref, sc_ref, ss_ref, sct_ref, sst_ref,
                       f_ref, g_ref):
    sc, ss = sc_ref[...], ss_ref[...]
    bc, bs = bc_ref[...], bs_ref[...]
    f_ref[:FREQ_CHUNK, :] = (sc * bc - ss * bs).astype(f_ref.dtype)
    f_ref[FREQ_CHUNK:, :] = (ss * bc + sc * bs).astype(f_ref.dtype)
    sct, sst = sct_ref[...], sst_ref[...]
    btc, bts = btc_ref[...], bts_ref[...]
    g_ref[:, :FREQ_CHUNK] = (sct * btc - sst * bts).astype(g_ref.dtype)
    g_ref[:, FREQ_CHUNK:] = (sst * btc + sct * bts).astype(g_ref.dtype)


def _dft_tables():
    def trig(k, t):
        ang = ((k * t) % DFT_N).astype(F32) * (2.0 * math.pi / DFT_N)
        return jnp.cos(ang), jnp.sin(ang)

    t = jnp.arange(L_PAD, dtype=jnp.int32)
    fine = jnp.arange(16, dtype=jnp.int32)
    coarse = jnp.arange(FREQ_CHUNK // 16, dtype=jnp.int32) * 16
    fc, fs = trig(fine[:, None], t[None, :])
    cc, cs = trig(coarse[:, None], t[None, :])
    bc = (cc[:, None] * fc[None] - cs[:, None] * fs[None]).reshape(FREQ_CHUNK, L_PAD)
    bs = (cs[:, None] * fc[None] + cc[:, None] * fs[None]).reshape(FREQ_CHUNK, L_PAD)
    live = (t < L_TRUE).astype(F32)[None, :]
    sc, ss = trig((jnp.arange(N_CHUNK, dtype=jnp.int32) * FREQ_CHUNK)[:, None], t[None, :])
    sc, ss = sc * live, ss * live
    full2 = lambda j: (0, 0)
    step_row = pl.BlockSpec((None, 1, L_PAD), lambda j: (j, 0, 0))
    step_col = pl.BlockSpec((None, L_PAD, 1), lambda j: (j, 0, 0))
    return pl.pallas_call(
        _dft_tables_kernel,
        grid=(N_CHUNK,),
        in_specs=[pl.BlockSpec((FREQ_CHUNK, L_PAD), full2), pl.BlockSpec((FREQ_CHUNK, L_PAD), full2),
                  pl.BlockSpec((L_PAD, FREQ_CHUNK), full2), pl.BlockSpec((L_PAD, FREQ_CHUNK), full2),
                  step_row, step_row, step_col, step_col],
        out_specs=[pl.BlockSpec((None, 2 * FREQ_CHUNK, L_PAD), lambda j: (j, 0, 0)),
                   pl.BlockSpec((None, L_PAD, 2 * FREQ_CHUNK), lambda j: (j, 0, 0))],
        out_shape=[jax.ShapeDtypeStruct((N_CHUNK, 2 * FREQ_CHUNK, L_PAD), BF16),
                   jax.ShapeDtypeStruct((N_CHUNK, L_PAD, 2 * FREQ_CHUNK), BF16)],
        compiler_params=_cparams(("parallel",), 48 << 20),
    )(bc, bs, bc.T, bs.T, sc[:, None, :], ss[:, None, :], sc[:, :, None], ss[:, :, None])


def _spectra_kernel(f_ref, h_ref, o_ref):
    k = pl.program_id(1) * FREQ_CHUNK + lax.broadcasted_iota(jnp.int32, (FREQ_CHUNK, 1), 0)
    wk = jnp.where(k == 0, 1.0 / DFT_N, 2.0 / DFT_N)
    o_ref[:FREQ_CHUNK, :] = jnp.dot(f_ref[:FREQ_CHUNK, :], h_ref[0],
                                    preferred_element_type=F32) * wk
    o_ref[FREQ_CHUNK:, :] = jnp.dot(f_ref[FREQ_CHUNK:, :], h_ref[1],
                                    preferred_element_type=F32) * wk


def _spectra(fmat, hsd, tn=1024):
    ncol = hsd.shape[-1]
    return pl.pallas_call(
        _spectra_kernel,
        grid=(ncol // tn, N_CHUNK),
        in_specs=[pl.BlockSpec((None, 2 * FREQ_CHUNK, L_PAD), lambda n, f: (f, 0, 0)),
                  pl.BlockSpec((2, L_PAD, tn), lambda n, f: (0, 0, n))],
        out_specs=pl.BlockSpec((None, 2 * FREQ_CHUNK, tn), lambda n, f: (f, 0, n)),
        out_shape=jax.ShapeDtypeStruct((N_CHUNK, 2 * FREQ_CHUNK, ncol), F32),
        compiler_params=_cparams(("parallel", "arbitrary"), 40 << 20),
    )(fmat, hsd)


def _short_conv3(x, w):
    return (pltpu.roll(x, 1, 0) * w[0:1] + x * w[1:2] + pltpu.roll(x, L_PAD - 1, 0) * w[2:3])


def _hyena_order_kernel(first, z_ref, g_ref, wz_ref, wg_ref, skip_ref, f_ref, gm_ref, pq_ref,
                        o_ref, zs_ref, zf_ref, acc_ref):
    fk = pl.program_id(2)

    @pl.when(fk == 0)
    def _():
        z = z_ref[...].astype(F32)
        if first:
            z = _short_conv3(_to_sequence_order(z), wz_ref[...])
        zf_ref[...] = z
        zs_ref[...] = z.astype(BF16)
        acc_ref[...] = jnp.zeros_like(acc_ref)

    ab = jnp.dot(f_ref[...], zs_ref[...], preferred_element_type=F32)
    a, b = ab[:FREQ_CHUNK], ab[FREQ_CHUNK:]
    p, q = pq_ref[:FREQ_CHUNK, :], pq_ref[FREQ_CHUNK:, :]
    uv = jnp.concatenate([(a * p - b * q).astype(BF16), (a * q + b * p).astype(BF16)], axis=0)
    acc_ref[...] += jnp.dot(gm_ref[...], uv, preferred_element_type=F32)

    @pl.when(fk == pl.num_programs(2) - 1)
    def _():
        gate = _short_conv3(_to_sequence_order(g_ref[...].astype(F32)), wg_ref[...])
        y = gate * (acc_ref[...] + skip_ref[...] * zf_ref[...])
        if first:
            o_ref[...] = y
        else:
            _store_from_sequence_order(o_ref, y)


def _hyena_order(order, zin, zin_col0, proj, conv_w, skip, fmat, gmat, pq, batch, tc=512):
    first = order == 0
    m = proj.shape[0]
    nct = BRANCH_W // tc
    zcb = zin_col0 // tc
    gcb = (COL_B + (order + 1) * BRANCH_W) // tc
    in_specs = [
        pl.BlockSpec((L_PAD, tc), lambda b, c, f: (b, zcb + c)),
        pl.BlockSpec((L_PAD, tc), lambda b, c, f: (b, gcb + c)),
        pl.BlockSpec((3, tc), lambda b, c, f: (0, c)),
        pl.BlockSpec((3, tc), lambda b, c, f: (0, (order + 1) * nct + c)),
        pl.BlockSpec((1, tc), lambda b, c, f: (0, c)),
        pl.BlockSpec((None, 2 * FREQ_CHUNK, L_PAD), lambda b, c, f: (f, 0, 0)),
        pl.BlockSpec((None, L_PAD, 2 * FREQ_CHUNK), lambda b, c, f: (f, 0, 0)),
        pl.BlockSpec((None, 2 * FREQ_CHUNK, tc), lambda b, c, f: (f, 0, order * nct + c)),
    ]
    return pl.pallas_call(
        functools.partial(_hyena_order_kernel, first),
        grid=(batch, nct, N_CHUNK),
        in_specs=in_specs,
        out_specs=pl.BlockSpec((L_PAD, tc), lambda b, c, f: (b, c)),
        out_shape=jax.ShapeDtypeStruct((m, BRANCH_W), F32),
        scratch_shapes=[pltpu.VMEM((L_PAD, tc), BF16),
                        pltpu.VMEM((L_PAD, tc), F32),
                        pltpu.VMEM((L_PAD, tc), F32)],
        compiler_params=_cparams(("parallel", "parallel", "arbitrary"), VMEM_CAP),
    )(zin, proj, conv_w, conv_w, skip[order].reshape(1, BRANCH_W), fmat, gmat, pq)


def _pool_kernel(p0_ref, p1_ref, p2_ref, p3_ref, gw_ref, scale_ref, o_ref):
    t = lax.broadcasted_iota(jnp.int32, (L_PAD, 1), 0)
    for g, (w, p_ref) in enumerate(zip(POOL_WINDOWS, (p0_ref, p1_ref, p2_ref, p3_ref))):
        p = _to_sequence_order(p_ref[...].astype(F32))
        s = p
        span = 1
        while span < w:
            s = s + pltpu.roll(s, L_PAD - span, 0)
            span *= 2
        back = (w - 1) // 2
        if back:
            s = pltpu.roll(s, back, 0)
        lo = jnp.clip(t - back, 0, L_TRUE)
        hi = jnp.clip(t + w // 2 + 1, 0, L_TRUE)
        cnt = jnp.maximum(hi - lo, 1).astype(F32)
        d = (s / cnt - p).astype(BF16)
        y = jnp.dot(d, gw_ref[g].astype(BF16), preferred_element_type=F32)
        sl = slice(g * POOL_GROUP_W, (g + 1) * POOL_GROUP_W)
        _store_from_sequence_order(o_ref, y * scale_ref[:, sl], sl)


def _pool(proj, group_w, scale, batch):
    m = proj.shape[0]
    cb = COL_C // POOL_GROUP_W
    grp = lambda g: pl.BlockSpec((L_PAD, POOL_GROUP_W), lambda b: (b, cb + g))
    return pl.pallas_call(
        _pool_kernel,
        grid=(batch,),
        in_specs=[grp(0), grp(1), grp(2), grp(3),
                  pl.BlockSpec((4, POOL_GROUP_W, POOL_GROUP_W), lambda b: (0, 0, 0)),
                  pl.BlockSpec((1, BRANCH_W), lambda b: (0, 0))],
        out_specs=pl.BlockSpec((L_PAD, BRANCH_W), lambda b: (b, 0)),
        out_shape=jax.ShapeDtypeStruct((m, BRANCH_W), BF16),
        compiler_params=_cparams(("parallel",), 48 << 20),
    )(proj, proj, proj, proj, group_w, scale.reshape(1, -1))


GATE_ROWS = 16


def _gate_kernel(tm, oa_ref, zb_ref, yc_ref, od_ref, gate_ref, na_ref, nb_ref, y_ref):
    branches = ((oa_ref, na_ref), (zb_ref, nb_ref), (yc_ref, None), (od_ref, None))

    def row_group(r, carry):
        row0 = pl.multiple_of(r * GATE_ROWS, GATE_ROWS)
        rows = pl.ds(row0, GATE_ROWS)
        pos = (pl.program_id(1) * tm + row0
               + lax.broadcasted_iota(jnp.int32, (GATE_ROWS, 1), 0))
        live = pos < L_TRUE
        for i, (x_ref, norm_ref) in enumerate(branches):
            sl = slice(i * BRANCH_W, (i + 1) * BRANCH_W)
            y = x_ref[rows, :].astype(F32)
            if norm_ref is not None:
                ms = jnp.mean(y * y, axis=-1, keepdims=True)
                y = y * lax.rsqrt(ms + EPS) * norm_ref[...]
            g = gate_ref[rows, sl].astype(F32)
            y = y * (g * jax.nn.sigmoid(g))
            y_ref[rows, sl] = jnp.where(live, y, 0.0).astype(y_ref.dtype)
        return carry

    lax.fori_loop(0, tm // GATE_ROWS, row_group, 0)


def _gate(oa, zb, yc, od, gate, na, nb, batch, tm=272):
    m = oa.shape[0]
    nt = L_PAD // tm
    row = lambda b, t: (b * nt + t, 0)
    full = lambda b, t: (0, 0)
    br = pl.BlockSpec((tm, BRANCH_W), row)
    return pl.pallas_call(
        functools.partial(_gate_kernel, tm),
        grid=(batch, nt),
        in_specs=[br, br, br, br, pl.BlockSpec((tm, D_MODEL), row),
                  pl.BlockSpec((1, BRANCH_W), full), pl.BlockSpec((1, BRANCH_W), full)],
        out_specs=pl.BlockSpec((tm, D_MODEL), row),
        out_shape=jax.ShapeDtypeStruct((m, D_MODEL), BF16),
        compiler_params=_cparams(("parallel", "parallel"), 32 << 20),
    )(oa, zb, yc, od, gate, na.reshape(1, -1), nb.reshape(1, -1))


def _inv_freq(dim, theta):
    return theta ** (-jnp.arange(0, dim, 2, dtype=F32) / dim)


def _pad_rows(t):
    return jnp.pad(t, ((0, L_PAD - t.shape[0]), (0, 0)))


def _rope_tables():
    rows = SEQ_REAL // GRID_W
    zeros_meta = jnp.zeros((N_META,), F32)
    row = jnp.concatenate([zeros_meta, jnp.repeat(jnp.arange(rows, dtype=F32), GRID_W)])
    col = jnp.concatenate([zeros_meta, jnp.tile(jnp.arange(GRID_W, dtype=F32), rows)])
    axf = _inv_freq(HEAD_DIM // 2, AXIAL_THETA)
    ar = axf[:, None] * row[None, :]
    ac = axf[:, None] * col[None, :]
    cos_a = jnp.concatenate([jnp.cos(ar), jnp.cos(ar), jnp.cos(ac), jnp.cos(ac)], axis=0)
    sin_a = jnp.concatenate([-jnp.sin(ar), jnp.sin(ar), -jnp.sin(ac), jnp.sin(ac)], axis=0)
    a1 = _inv_freq(ROPE_DIMS, ROPE_THETA)[:, None] * jnp.arange(L_TRUE, dtype=F32)[None, :]
    rest = HEAD_DIM - ROPE_DIMS
    cos_d = jnp.concatenate([jnp.cos(a1), jnp.cos(a1), jnp.ones((rest, L_TRUE), F32)], axis=0)
    sin_d = jnp.concatenate([-jnp.sin(a1), jnp.sin(a1), jnp.zeros((rest, L_TRUE), F32)], axis=0)

    def hbm_rows(tab_t):
        tab = tab_t.T
        return _pad_rows(jnp.concatenate([tab[N_META:], tab[:N_META]], axis=0))

    return hbm_rows(cos_a), hbm_rows(sin_a), hbm_rows(cos_d), hbm_rows(sin_d)


def _hyena_tables():
    t = jnp.linspace(0.0, 1.0, L_TRUE, dtype=F32)[:, None]
    bands = (HYENA_EMB_DIM - 1) // 2
    fb = jnp.linspace(1e-4, bands - 1, bands, dtype=F32)[:, None]
    wpos = 2.0 * math.pi * jnp.arange(L_TRUE, dtype=F32)[None, :] / L_TRUE
    ang = fb * wpos
    z = jnp.concatenate([t.T, jnp.cos(ang), -jnp.sin(ang)], axis=0).T
    z = jnp.pad(z, ((0, L_PAD - L_TRUE), (0, LANE - HYENA_EMB_DIM)))
    deltas = jnp.abs(jnp.linspace(math.log(HYENA_TARGET) / HYENA_SLOW,
                                  math.log(HYENA_TARGET) / HYENA_FAST, BRANCH_W, dtype=F32))
    decay = _pad_rows(jnp.exp(-t * deltas) + HYENA_SHIFT)
    return z, decay


def _pad2(a, rows, cols):
    return jnp.pad(a, ((0, rows - a.shape[0]), (0, cols - a.shape[1])))


def kernel(x, meta_tokens, norm_w, w_in, w_out, a_q_norm, a_k_norm, a_out_norm, b_short_conv,
           b_filt_w1, b_filt_b1, b_filt_w2, b_filt_b2, b_filt_w3, b_sin_freq, b_skip, b_out_norm,
           c_group_w, c_scale, d_q_norm, d_k_norm, d_lambda, d_out_norm):
    batch = x.shape[0]
    depth = norm_w.shape[0]
    m = batch * L_PAD
    tail = _pad2(meta_tokens.astype(x.dtype), L_PAD - SEQ_REAL, D_MODEL)

    cos_a, sin_a, cos_d, sin_d = _rope_tables()
    fmat, gmat = _dft_tables()
    zfeat, decay = _hyena_tables()
    bias = jnp.where(jnp.arange(L_PAD) < L_TRUE, 0.0, NEG_BIG).astype(F32).reshape(1, L_PAD)

    for l in range(depth):
        lam_init = 0.8 - 0.6 * math.exp(-0.3 * l)
        if l == 0:
            h, u = _embed_rmsnorm(x, tail, norm_w[0])
            h, u = h.reshape(m, D_MODEL), u.reshape(m, D_MODEL)
        else:
            u = _rmsnorm(h, norm_w[l])
        proj = _matmul(u, w_in, l, 0, MAIN_W, BF16)
        gate = _matmul(u, w_in, l, COL_GATE, D_MODEL, BF16)

        oa = _attn_a(proj, cos_a, sin_a, a_q_norm[l], a_k_norm[l], bias, batch)
        od = _attn_d(proj, cos_d, sin_d, d_q_norm[l], d_k_norm[l], d_lambda[l], d_out_norm[l],
                     bias, lam_init, batch)

        hsd = _hyena_filter(
            zfeat,
            _pad2(b_filt_w1[l], LANE, LANE), _pad2(b_filt_b1[l][None], 1, LANE),
            _pad2(b_filt_w2[l], LANE, LANE), _pad2(b_filt_b2[l][None], 1, LANE),
            _pad2(b_sin_freq[l][None], 1, LANE), _pad2(b_filt_w3[l], LANE, 4 * BRANCH_W), decay)
        pq = _spectra(fmat, hsd)
        z1 = _hyena_order(0, proj, COL_B, proj, b_short_conv[l], b_skip[l], fmat, gmat, pq, batch)
        z2 = _hyena_order(1, z1, 0, proj, b_short_conv[l], b_skip[l], fmat, gmat, pq, batch)

        yc = _pool(proj, c_group_w[l], c_scale[l], batch)
        y = _gate(oa, z2, yc, od, gate, a_out_norm[l], b_out_norm[l], batch)
        if l < depth - 1:
            h = _matmul(y, w_out, l, 0, D_MODEL, F32, residual=h)
        else:
            out = _matmul_final(y.reshape(batch, L_PAD, D_MODEL), w_out, l,
                                h.reshape(batch, L_PAD, D_MODEL))
    return out
```

```python
import functools
import math

import jax
import jax.numpy as jnp
from jax import lax
from jax.experimental import pallas as pl
from jax.experimental.pallas import tpu as pltpu

F32 = jnp.float32
BF16 = jnp.bfloat16

D_MODEL = 4096
N_META = 16
GRID_W = 64
EPS = 1e-6
BRANCH_W = 1024
HEAD_DIM = 128
A_HEADS = 8
A_KV_HEADS = 2
A_GROUPS = A_HEADS // A_KV_HEADS
AXIAL_THETA = 10000.0
HYENA_EMB_DIM = 33
HYENA_FFN = 64
HYENA_FAST = 0.3
HYENA_SLOW = 1.5
HYENA_TARGET = 1e-2
HYENA_SHIFT = 0.05
POOL_WINDOWS = (2, 4, 8, 16)
POOL_GROUP_W = 256
D_HEADS = 4
D_HEAD_V = 256
ROPE_THETA = 500000.0
ROPE_DIMS = 32

COL_AQ = 0
COL_AK = 1024
COL_AV = 1280
COL_B = 1536
COL_C = 4608
COL_DQ = 5632
COL_DK = 6656
COL_DV = 7680
COL_GATE = 8704
MAIN_W = COL_GATE
IN_COLS = 12800

LANE = 128
SEQ_REAL = 2048
L_TRUE = N_META + SEQ_REAL
L_PAD = 2176
DFT_N = 4607
N_FREQ = 2304
FREQ_CHUNK = 384
N_CHUNK = N_FREQ // FREQ_CHUNK
NEG_BIG = -1e30
LOG2E = 1.4426950408889634
Q_SCALE = (HEAD_DIM ** -0.5) * LOG2E
assert L_TRUE > L_PAD - LANE
VMEM_CAP = 56 * 1024 * 1024


def _cparams(semantics, vmem_bytes):
    return pltpu.CompilerParams(dimension_semantics=semantics,
                                vmem_limit_bytes=min(int(vmem_bytes), VMEM_CAP))


def _to_sequence_order(x):
    return jnp.concatenate([x[SEQ_REAL:L_TRUE], x[:SEQ_REAL], x[L_TRUE:]], axis=0)


def _store_from_sequence_order(o_ref, y, cols=slice(None)):
    o_ref[:SEQ_REAL, cols] = y[N_META:L_TRUE].astype(o_ref.dtype)
    o_ref[SEQ_REAL:L_TRUE, cols] = y[:N_META].astype(o_ref.dtype)
    o_ref[L_TRUE:, cols] = y[L_TRUE:].astype(o_ref.dtype)


def _rmsnorm_kernel(h_ref, w_ref, o_ref):
    x = h_ref[...]
    ms = jnp.mean(x * x, axis=-1, keepdims=True)
    o_ref[...] = (x * lax.rsqrt(ms + EPS) * w_ref[...]).astype(o_ref.dtype)


def _embed_rmsnorm_kernel(n_real_tiles, x_ref, tail_ref, w_ref, h_ref, u_ref):
    def emit(x):
        h_ref[...] = x
        ms = jnp.mean(x * x, axis=-1, keepdims=True)
        u_ref[...] = (x * lax.rsqrt(ms + EPS) * w_ref[...]).astype(u_ref.dtype)

    @pl.when(pl.program_id(1) < n_real_tiles)
    def _():
        emit(x_ref[...])

    @pl.when(pl.program_id(1) == n_real_tiles)
    def _():
        emit(tail_ref[...])


def _embed_rmsnorm(x, tail, w, tm=LANE):
    batch, n_real, d = x.shape
    n_real_tiles = n_real // tm
    row = lambda b, i: (b, i, 0)
    return pl.pallas_call(
        functools.partial(_embed_rmsnorm_kernel, n_real_tiles),
        grid=(batch, L_PAD // tm),
        in_specs=[pl.BlockSpec((None, tm, d), lambda b, i: (b, jnp.minimum(i, n_real_tiles - 1), 0)),
                  pl.BlockSpec((tm, d), lambda b, i: (0, 0)),
                  pl.BlockSpec((1, d), lambda b, i: (0, 0))],
        out_specs=[pl.BlockSpec((None, tm, d), row), pl.BlockSpec((None, tm, d), row)],
        out_shape=[jax.ShapeDtypeStruct((batch, L_PAD, d), F32),
                   jax.ShapeDtypeStruct((batch, L_PAD, d), BF16)],
        compiler_params=_cparams(("parallel", "arbitrary"), 32 << 20),
    )(x, tail, w.reshape(1, d))


def _rmsnorm(h, w, tm=544):
    m, d = h.shape
    return pl.pallas_call(
        _rmsnorm_kernel,
        grid=(m // tm,),
        in_specs=[pl.BlockSpec((tm, d), lambda i: (i, 0)),
                  pl.BlockSpec((1, d), lambda i: (0, 0))],
        out_specs=pl.BlockSpec((tm, d), lambda i: (i, 0)),
        out_shape=jax.ShapeDtypeStruct((m, d), BF16),
        compiler_params=_cparams(("parallel",), 40 << 20),
    )(h, w.reshape(1, d))


def _matmul_kernel(x_ref, w_ref, o_ref, wb_ref):
    @pl.when(pl.program_id(1) == 0)
    def _():
        wb_ref[...] = w_ref[...].astype(BF16)

    o_ref[...] = jnp.dot(x_ref[...], wb_ref[...],
                         preferred_element_type=F32).astype(o_ref.dtype)


def _matmul_res_kernel(x_ref, w_ref, r_ref, o_ref, wb_ref):
    @pl.when(pl.program_id(1) == 0)
    def _():
        wb_ref[...] = w_ref[...].astype(BF16)

    o_ref[...] = r_ref[...] + jnp.dot(x_ref[...], wb_ref[...], preferred_element_type=F32)


def _matmul(x, w3, layer, col0, width, out_dtype, residual=None, tm=1088, tn=512):
    m, k = x.shape
    cb0 = col0 // tn
    grid = (width // tn, m // tm)
    in_specs = [pl.BlockSpec((tm, k), lambda j, i: (i, 0)),
                pl.BlockSpec((None, k, tn), lambda j, i: (layer, 0, cb0 + j))]
    args = [x, w3]
    kern = _matmul_kernel
    if residual is not None:
        in_specs.append(pl.BlockSpec((tm, tn), lambda j, i: (i, j)))
        args.append(residual)
        kern = _matmul_res_kernel
    vmem = 2 * (tm * k * 2 + k * tn * 4 + 2 * tm * tn * 4) + k * tn * 2 + (4 << 20)
    return pl.pallas_call(
        kern,
        grid=grid,
        in_specs=in_specs,
        out_specs=pl.BlockSpec((tm, tn), lambda j, i: (i, j)),
        out_shape=jax.ShapeDtypeStruct((m, width), out_dtype),
        scratch_shapes=[pltpu.VMEM((k, tn), BF16)],
        compiler_params=_cparams(("parallel", "arbitrary"), vmem),
    )(*args)


def _matmul_final_kernel(x_ref, w_ref, r_ref, o_ref, wb_ref):
    @pl.when((pl.program_id(1) == 0) & (pl.program_id(2) == 0))
    def _():
        wb_ref[...] = w_ref[...].astype(BF16)

    o_ref[...] = r_ref[...] + jnp.dot(x_ref[...], wb_ref[...], preferred_element_type=F32)


def _matmul_final(y3, w3, layer, h3, tm=1024, tn=512):
    batch, _, k = y3.shape
    d = h3.shape[-1]
    vmem = 2 * (tm * k * 2 + k * tn * 4 + 2 * tm * tn * 4) + k * tn * 2 + (4 << 20)
    return pl.pallas_call(
        _matmul_final_kernel,
        grid=(d // tn, batch, SEQ_REAL // tm),
        in_specs=[pl.BlockSpec((None, tm, k), lambda j, b, i: (b, i, 0)),
                  pl.BlockSpec((None, k, tn), lambda j, b, i: (layer, 0, j)),
                  pl.BlockSpec((None, tm, tn), lambda j, b, i: (b, i, j))],
        out_specs=pl.BlockSpec((None, tm, tn), lambda j, b, i: (b, i, j)),
        out_shape=jax.ShapeDtypeStruct((batch, SEQ_REAL, d), F32),
        scratch_shapes=[pltpu.VMEM((k, tn), BF16)],
        compiler_params=_cparams(("parallel", "arbitrary", "arbitrary"), vmem),
    )(y3, w3, h3)


def _norm_rope(x, gain, cos, sin_signed, half_rot):
    ms = jnp.mean(x * x, axis=-1, keepdims=True)
    xn = x * lax.rsqrt(ms + EPS) * gain
    lane = lax.broadcasted_iota(jnp.int32, (1, LANE), 1)
    partner_up = (lane & half_rot) == 0
    partner = jnp.where(partner_up,
                        pltpu.roll(xn, LANE - half_rot, 1),
                        pltpu.roll(xn, half_rot, 1))
    return xn * cos + partner * sin_signed


def _softmax_pv(q_bf, k_bf, v_bf, bias_last, denom_from_ones):
    s = lax.dot_general(q_bf, k_bf, (((1,), (1,)), ((), ())), preferred_element_type=F32)
    n_main = L_PAD - LANE
    s_main = s[:, :n_main]
    s_last = s[:, n_main:] + bias_last
    m = jnp.maximum(jnp.max(s_main, axis=-1, keepdims=True),
                    jnp.max(s_last, axis=-1, keepdims=True))
    p_main = jnp.exp2(s_main - m)
    p_last = jnp.exp2(s_last - m)
    p = jnp.concatenate([p_main.astype(BF16), p_last.astype(BF16)], axis=1)
    o = jnp.dot(p, v_bf, preferred_element_type=F32)
    if denom_from_ones:
        return o[:, :HEAD_DIM] * (1.0 / o[:, HEAD_DIM:HEAD_DIM + 1])
    l = jnp.sum(p_main, axis=-1, keepdims=True) + jnp.sum(p_last, axis=-1, keepdims=True)
    return o * (1.0 / l)


def _attn_a_kernel(sub, q_ref, k_ref, v_ref, cq_ref, sq_ref, ck_ref, sk_ref, gq_ref, gk_ref,
                   bias_ref, o_ref, kh_ref, vx_ref, qh_ref):
    @pl.when((pl.program_id(2) == 0) & (pl.program_id(3) == 0))
    def _():
        kh_ref[...] = _norm_rope(k_ref[...].astype(F32), gk_ref[...], ck_ref[...], sk_ref[...],
                                 HEAD_DIM // 4).astype(BF16)
        vx_ref[:, :HEAD_DIM] = v_ref[...]
        vx_ref[:, HEAD_DIM:] = jnp.ones((L_PAD, HEAD_DIM), BF16)

    q = _norm_rope(q_ref[...].astype(F32), gq_ref[...], cq_ref[...], sq_ref[...], HEAD_DIM // 4)
    qh_ref[...] = (q * Q_SCALE).astype(BF16)
    bias_last = bias_ref[:, L_PAD - LANE:]
    for i in range(q_ref.shape[0] // sub):
        rows = slice(i * sub, (i + 1) * sub)
        o_ref[rows, :] = _softmax_pv(qh_ref[rows, :], kh_ref[...], vx_ref[...], bias_last,
                                     True).astype(o_ref.dtype)


def _attn_a(proj, cos_t, sin_t, gq, gk, bias, batch, tq=L_PAD, sub=272):
    m = proj.shape[0]
    nq = L_PAD // tq
    qblk = lambda b, kv, g, qi: (b * nq + qi, COL_AQ // HEAD_DIM + kv * A_GROUPS + g)
    tabq = lambda b, kv, g, qi: (qi, 0)
    full = lambda b, kv, g, qi: (0, 0)
    return pl.pallas_call(
        functools.partial(_attn_a_kernel, sub),
        grid=(batch, A_KV_HEADS, A_GROUPS, nq),
        in_specs=[
            pl.BlockSpec((tq, HEAD_DIM), qblk),
            pl.BlockSpec((L_PAD, HEAD_DIM), lambda b, kv, g, qi: (b, COL_AK // HEAD_DIM + kv)),
            pl.BlockSpec((L_PAD, HEAD_DIM), lambda b, kv, g, qi: (b, COL_AV // HEAD_DIM + kv)),
            pl.BlockSpec((tq, HEAD_DIM), tabq),
            pl.BlockSpec((tq, HEAD_DIM), tabq),
            pl.BlockSpec((L_PAD, HEAD_DIM), full),
            pl.BlockSpec((L_PAD, HEAD_DIM), full),
            pl.BlockSpec((1, HEAD_DIM), full),
            pl.BlockSpec((1, HEAD_DIM), full),
            pl.BlockSpec((1, L_PAD), full),
        ],
        out_specs=pl.BlockSpec((tq, HEAD_DIM), lambda b, kv, g, qi: (b * nq + qi, kv * A_GROUPS + g)),
        out_shape=jax.ShapeDtypeStruct((m, BRANCH_W), BF16),
        scratch_shapes=[pltpu.VMEM((L_PAD, HEAD_DIM), BF16),
                        pltpu.VMEM((L_PAD, 2 * HEAD_DIM), BF16),
                        pltpu.VMEM((tq, HEAD_DIM), BF16)],
        compiler_params=_cparams(("parallel", "parallel", "arbitrary", "arbitrary"), 48 << 20),
    )(proj, proj, proj, cos_t, sin_t, cos_t, sin_t, gq.reshape(1, -1), gk.reshape(1, -1), bias)


def _attn_d_kernel(lam_init, sub, q_ref, k_ref, v_ref, cq_ref, sq_ref, ck_ref, sk_ref, gq_ref,
                   gk_ref, lam_ref, gout_ref, bias_ref, o_ref, kh_ref, qh_ref):
    halves = (slice(0, HEAD_DIM), slice(HEAD_DIM, 2 * HEAD_DIM))

    @pl.when(pl.program_id(2) == 0)
    def _():
        for sl in halves:
            kh_ref[:, sl] = _norm_rope(k_ref[:, sl].astype(F32), gk_ref[...], ck_ref[...],
                                       sk_ref[...], ROPE_DIMS // 2).astype(BF16)

    for sl in halves:
        q = _norm_rope(q_ref[:, sl].astype(F32), gq_ref[...], cq_ref[...], sq_ref[...],
                       ROPE_DIMS // 2)
        qh_ref[:, sl] = (q * Q_SCALE).astype(BF16)
    lf = lam_ref[...]
    lam = (jnp.exp(jnp.sum(lf[0:1] * lf[1:2], axis=-1, keepdims=True))
           - jnp.exp(jnp.sum(lf[2:3] * lf[3:4], axis=-1, keepdims=True)) + lam_init)
    bias_last = bias_ref[:, L_PAD - LANE:]
    for i in range(q_ref.shape[0] // sub):
        rows = slice(i * sub, (i + 1) * sub)
        o1, o2 = [_softmax_pv(qh_ref[rows, sl], kh_ref[:, sl], v_ref[...], bias_last, False)
                  for sl in halves]
        o = o1 - lam * o2
        ms = jnp.mean(o * o, axis=-1, keepdims=True)
        o = o * lax.rsqrt(ms + EPS) * gout_ref[...] * (1.0 - lam_init)
        o_ref[rows, :] = o.astype(o_ref.dtype)


def _attn_d(proj, cos_t, sin_t, gq, gk, lam_vec, gout, bias, lam_init, batch, tq=1088, sub=272):
    m = proj.shape[0]
    nq = L_PAD // tq
    w = 2 * HEAD_DIM
    tabq = lambda b, h, qi: (qi, 0)
    full = lambda b, h, qi: (0, 0)
    return pl.pallas_call(
        functools.partial(_attn_d_kernel, lam_init, sub),
        grid=(batch, D_HEADS, nq),
        in_specs=[
            pl.BlockSpec((tq, w), lambda b, h, qi: (b * nq + qi, COL_DQ // w + h)),
            pl.BlockSpec((L_PAD, w), lambda b, h, qi: (b, COL_DK // w + h)),
            pl.BlockSpec((L_PAD, w), lambda b, h, qi: (b, COL_DV // w + h)),
            pl.BlockSpec((tq, HEAD_DIM), tabq),
            pl.BlockSpec((tq, HEAD_DIM), tabq),
            pl.BlockSpec((L_PAD, HEAD_DIM), full),
            pl.BlockSpec((L_PAD, HEAD_DIM), full),
            pl.BlockSpec((1, HEAD_DIM), full),
            pl.BlockSpec((1, HEAD_DIM), full),
            pl.BlockSpec((4, HEAD_DIM), full),
            pl.BlockSpec((1, w), full),
            pl.BlockSpec((1, L_PAD), full),
        ],
        out_specs=pl.BlockSpec((tq, w), lambda b, h, qi: (b * nq + qi, h)),
        out_shape=jax.ShapeDtypeStruct((m, BRANCH_W), BF16),
        scratch_shapes=[pltpu.VMEM((L_PAD, w), BF16), pltpu.VMEM((tq, w), BF16)],
        compiler_params=_cparams(("parallel", "parallel", "arbitrary"), 48 << 20),
    )(proj, proj, proj, cos_t, sin_t, cos_t, sin_t, gq.reshape(1, -1), gk.reshape(1, -1),
      lam_vec, gout.reshape(1, -1), bias)


def _hyena_filter_kernel(z_ref, w1_ref, b1_ref, w2_ref, b2_ref, fr_ref, w3f_ref, w3b_ref,
                         dec_ref, o_ref, hid_ref):
    hp = lax.Precision.HIGHEST

    @pl.when((pl.program_id(0) == 0) & (pl.program_id(1) == 0))
    def _():
        fr = fr_ref[...]
        h1 = jnp.sin(fr * (jnp.dot(z_ref[...], w1_ref[...], precision=hp,
                                   preferred_element_type=F32) + b1_ref[...]))
        hid_ref[...] = jnp.sin(fr * (jnp.dot(h1, w2_ref[...], precision=hp,
                                             preferred_element_type=F32) + b2_ref[...]))

    hid = hid_ref[...]
    dec = dec_ref[...]
    lag = lax.broadcasted_iota(jnp.int32, (L_PAD, 1), 0)
    hf = jnp.dot(hid, w3f_ref[...], precision=hp, preferred_element_type=F32) * dec
    hb = jnp.dot(hid, w3b_ref[...], precision=hp, preferred_element_type=F32) * dec
    hf = jnp.where(lag < L_TRUE, hf, 0.0)
    hb = jnp.where((lag >= 1) & (lag < L_TRUE), hb, 0.0)
    o_ref[0] = (hf + hb).astype(o_ref.dtype)
    o_ref[1] = (hf - hb).astype(o_ref.dtype)


def _hyena_filter(zfeat, w1, b1, w2, b2, fr, w3, decay, tc=512):
    nct = BRANCH_W // tc
    full = lambda o, c: (0, 0)
    return pl.pallas_call(
        _hyena_filter_kernel,
        grid=(2, nct),
        in_specs=[
            pl.BlockSpec((L_PAD, LANE), full),
            pl.BlockSpec((LANE, LANE), full),
            pl.BlockSpec((1, LANE), full),
            pl.BlockSpec((LANE, LANE), full),
            pl.BlockSpec((1, LANE), full),
            pl.BlockSpec((1, LANE), full),
            pl.BlockSpec((LANE, tc), lambda o, c: (0, (o * 2) * nct + c)),
            pl.BlockSpec((LANE, tc), lambda o, c: (0, (o * 2 + 1) * nct + c)),
            pl.BlockSpec((L_PAD, tc), lambda o, c: (0, c)),
        ],
        out_specs=pl.BlockSpec((2, L_PAD, tc), lambda o, c: (0, 0, o * nct + c)),
        out_shape=jax.ShapeDtypeStruct((2, L_PAD, 2 * BRANCH_W), BF16),
        scratch_shapes=[pltpu.VMEM((L_PAD, LANE), F32)],
        compiler_params=_cparams(("arbitrary", "arbitrary"), 48 << 20),
    )(zfeat, w1, b1, w2, b2, fr, w3, w3, decay)


def _dft_tables_kernel(bc_ref, bs_ref, btc_ref, bts_ref, sc_ref, ss_ref, sct_ref, sst_ref,
                       f_ref, g_ref):
    sc, ss = sc_ref[...], ss_ref[...]
    bc, bs = bc_ref[...], bs_ref[...]
    f_ref[:FREQ_CHUNK, :] = (sc * bc - ss * bs).astype(f_ref.dtype)
    f_ref[FREQ_CHUNK:, :] = (ss * bc + sc * bs).astype(f_ref.dtype)
    sct, sst = sct_ref[...], sst_ref[...]
    btc, bts = btc_ref[...], bts_ref[...]
    g_ref[:, :FREQ_CHUNK] = (sct * btc - sst * bts).astype(g_ref.dtype)
    g_ref[:, FREQ_CHUNK:] = (sst * btc + sct * bts).astype(g_ref.dtype)


def _dft_tables():
    def trig(k, t):
        ang = ((k * t) % DFT_N).astype(F32) * (2.0 * math.pi / DFT_N)
        return jnp.cos(ang), jnp.sin(ang)

    t = jnp.arange(L_PAD, dtype=jnp.int32)
    fine = jnp.arange(16, dtype=jnp.int32)
    coarse = jnp.arange(FREQ_CHUNK // 16, dtype=jnp.int32) * 16
    fc, fs = trig(fine[:, None], t[None, :])
    cc, cs = trig(coarse[:, None], t[None, :])
    bc = (cc[:, None] * fc[None] - cs[:, None] * fs[None]).reshape(FREQ_CHUNK, L_PAD)
    bs = (cs[:, None] * fc[None] + cc[:, None] * fs[None]).reshape(FREQ_CHUNK, L_PAD)
    live = (t < L_TRUE).astype(F32)[None, :]
    sc, ss = trig((jnp.arange(N_CHUNK, dtype=jnp.int32) * FREQ_CHUNK)[:, None], t[None, :])
    sc, ss = sc * live, ss * live
    full2 = lambda j: (0, 0)
    step_row = pl.BlockSpec((None, 1, L_PAD), lambda j: (j, 0, 0))
    step_col = pl.BlockSpec((None, L_PAD, 1), lambda j: (j, 0, 0))
    return pl.pallas_call(
        _dft_tables_kernel,
        grid=(N_CHUNK,),
        in_specs=[pl.BlockSpec((FREQ_CHUNK, L_PAD), full2), pl.BlockSpec((FREQ_CHUNK, L_PAD), full2),
                  pl.BlockSpec((L_PAD, FREQ_CHUNK), full2), pl.BlockSpec((L_PAD, FREQ_CHUNK), full2),
                  step_row, step_row, step_col, step_col],
        out_specs=[pl.BlockSpec((None, 2 * FREQ_CHUNK, L_PAD), lambda j: (j, 0, 0)),
                   pl.BlockSpec((None, L_PAD, 2 * FREQ_CHUNK), lambda j: (j, 0, 0))],
        out_shape=[jax.ShapeDtypeStruct((N_CHUNK, 2 * FREQ_CHUNK, L_PAD), BF16),
                   jax.ShapeDtypeStruct((N_CHUNK, L_PAD, 2 * FREQ_CHUNK), BF16)],
        compiler_params=_cparams(("parallel",), 48 << 20),
    )(bc, bs, bc.T, bs.T, sc[:, None, :], ss[:, None, :], sc[:, :, None], ss[:, :, None])


def _spectra_kernel(f_ref, h_ref, o_ref):
    k = pl.program_id(1) * FREQ_CHUNK + lax.broadcasted_iota(jnp.int32, (FREQ_CHUNK, 1), 0)
    wk = jnp.where(k == 0, 1.0 / DFT_N, 2.0 / DFT_N)
    o_ref[:FREQ_CHUNK, :] = jnp.dot(f_ref[:FREQ_CHUNK, :], h_ref[0],
                                    preferred_element_type=F32) * wk
    o_ref[FREQ_CHUNK:, :] = jnp.dot(f_ref[FREQ_CHUNK:, :], h_ref[1],
                                    preferred_element_type=F32) * wk


def _spectra(fmat, hsd, tn=1024):
    ncol = hsd.shape[-1]
    return pl.pallas_call(
        _spectra_kernel,
        grid=(ncol // tn, N_CHUNK),
        in_specs=[pl.BlockSpec((None, 2 * FREQ_CHUNK, L_PAD), lambda n, f: (f, 0, 0)),
                  pl.BlockSpec((2, L_PAD, tn), lambda n, f: (0, 0, n))],
        out_specs=pl.BlockSpec((None, 2 * FREQ_CHUNK, tn), lambda n, f: (f, 0, n)),
        out_shape=jax.ShapeDtypeStruct((N_CHUNK, 2 * FREQ_CHUNK, ncol), F32),
        compiler_params=_cparams(("parallel", "arbitrary"), 40 << 20),
    )(fmat, hsd)


def _short_conv3(x, w):
    return (pltpu.roll(x, 1, 0) * w[0:1] + x * w[1:2] + pltpu.roll(x, L_PAD - 1, 0) * w[2:3])


def _hyena_order_kernel(first, z_ref, g_ref, wz_ref, wg_ref, skip_ref, f_ref, gm_ref, pq_ref,
                        o_ref, zs_ref, zf_ref, acc_ref):
    fk = pl.program_id(2)

    @pl.when(fk == 0)
    def _():
        z = z_ref[...].astype(F32)
        if first:
            z = _short_conv3(_to_sequence_order(z), wz_ref[...])
        zf_ref[...] = z
        zs_ref[...] = z.astype(BF16)
        acc_ref[...] = jnp.zeros_like(acc_ref)

    ab = jnp.dot(f_ref[...], zs_ref[...], preferred_element_type=F32)
    a, b = ab[:FREQ_CHUNK], ab[FREQ_CHUNK:]
    p, q = pq_ref[:FREQ_CHUNK, :], pq_ref[FREQ_CHUNK:, :]
    uv = jnp.concatenate([(a * p - b * q).astype(BF16), (a * q + b * p).astype(BF16)], axis=0)
    acc_ref[...] += jnp.dot(gm_ref[...], uv, preferred_element_type=F32)

    @pl.when(fk == pl.num_programs(2) - 1)
    def _():
        gate = _short_conv3(_to_sequence_order(g_ref[...].astype(F32)), wg_ref[...])
        y = gate * (acc_ref[...] + skip_ref[...] * zf_ref[...])
        if first:
            o_ref[...] = y
        else:
            _store_from_sequence_order(o_ref, y)


def _hyena_order(order, zin, zin_col0, proj, conv_w, skip, fmat, gmat, pq, batch, tc=512):
    first = order == 0
    m = proj.shape[0]
    nct = BRANCH_W // tc
    zcb = zin_col0 // tc
    gcb = (COL_B + (order + 1) * BRANCH_W) // tc
    in_specs = [
        pl.BlockSpec((L_PAD, tc), lambda b, c, f: (b, zcb + c)),
        pl.BlockSpec((L_PAD, tc), lambda b, c, f: (b, gcb + c)),
        pl.BlockSpec((3, tc), lambda b, c, f: (0, c)),
        pl.BlockSpec((3, tc), lambda b, c, f: (0, (order + 1) * nct + c)),
        pl.BlockSpec((1, tc), lambda b, c, f: (0, c)),
        pl.BlockSpec((None, 2 * FREQ_CHUNK, L_PAD), lambda b, c, f: (f, 0, 0)),
        pl.BlockSpec((None, L_PAD, 2 * FREQ_CHUNK), lambda b, c, f: (f, 0, 0)),
        pl.BlockSpec((None, 2 * FREQ_CHUNK, tc), lambda b, c, f: (f, 0, order * nct + c)),
    ]
    return pl.pallas_call(
        functools.partial(_hyena_order_kernel, first),
        grid=(batch, nct, N_CHUNK),
        in_specs=in_specs,
        out_specs=pl.BlockSpec((L_PAD, tc), lambda b, c, f: (b, c)),
        out_shape=jax.ShapeDtypeStruct((m, BRANCH_W), F32 if first else BF16),
        scratch_shapes=[pltpu.VMEM((L_PAD, tc), BF16),
                        pltpu.VMEM((L_PAD, tc), F32),
                        pltpu.VMEM((L_PAD, tc), F32)],
        compiler_params=_cparams(("parallel", "parallel", "arbitrary"), VMEM_CAP),
    )(zin, proj, conv_w, conv_w, skip[order].reshape(1, BRANCH_W), fmat, gmat, pq)


def _pool_kernel(p0_ref, p1_ref, p2_ref, p3_ref, gw_ref, scale_ref, o_ref):
    t = lax.broadcasted_iota(jnp.int32, (L_PAD, 1), 0)
    for g, (w, p_ref) in enumerate(zip(POOL_WINDOWS, (p0_ref, p1_ref, p2_ref, p3_ref))):
        p = _to_sequence_order(p_ref[...].astype(F32))
        s = p
        span = 1
        while span < w:
            s = s + pltpu.roll(s, L_PAD - span, 0)
            span *= 2
        back = (w - 1) // 2
        if back:
            s = pltpu.roll(s, back, 0)
        lo = jnp.clip(t - back, 0, L_TRUE)
        hi = jnp.clip(t + w // 2 + 1, 0, L_TRUE)
        cnt = jnp.maximum(hi - lo, 1).astype(F32)
        d = (s / cnt - p).astype(BF16)
        y = jnp.dot(d, gw_ref[g].astype(BF16), preferred_element_type=F32)
        sl = slice(g * POOL_GROUP_W, (g + 1) * POOL_GROUP_W)
        _store_from_sequence_order(o_ref, y * scale_ref[:, sl], sl)


def _pool(proj, group_w, scale, batch):
    m = proj.shape[0]
    cb = COL_C // POOL_GROUP_W
    grp = lambda g: pl.BlockSpec((L_PAD, POOL_GROUP_W), lambda b: (b, cb + g))
    return pl.pallas_call(
        _pool_kernel,
        grid=(batch,),
        in_specs=[grp(0), grp(1), grp(2), grp(3),
                  pl.BlockSpec((4, POOL_GROUP_W, POOL_GROUP_W), lambda b: (0, 0, 0)),
                  pl.BlockSpec((1, BRANCH_W), lambda b: (0, 0))],
        out_specs=pl.BlockSpec((L_PAD, BRANCH_W), lambda b: (b, 0)),
        out_shape=jax.ShapeDtypeStruct((m, BRANCH_W), BF16),
        compiler_params=_cparams(("parallel",), 48 << 20),
    )(proj, proj, proj, proj, group_w, scale.reshape(1, -1))


GATE_ROWS = 16


def _gate_kernel(tm, oa_ref, zb_ref, yc_ref, od_ref, gate_ref, na_ref, nb_ref, y_ref):
    branches = ((oa_ref, na_ref), (zb_ref, nb_ref), (yc_ref, None), (od_ref, None))

    def row_group(r, carry):
        row0 = pl.multiple_of(r * GATE_ROWS, GATE_ROWS)
        rows = pl.ds(row0, GATE_ROWS)
        pos = (pl.program_id(1) * tm + row0
               + lax.broadcasted_iota(jnp.int32, (GATE_ROWS, 1), 0))
        live = pos < L_TRUE
        for i, (x_ref, norm_ref) in enumerate(branches):
            sl = slice(i * BRANCH_W, (i + 1) * BRANCH_W)
            y = x_ref[rows, :].astype(F32)
            if norm_ref is not None:
                ms = jnp.mean(y * y, axis=-1, keepdims=True)
                y = y * lax.rsqrt(ms + EPS) * norm_ref[...]
            g = gate_ref[rows, sl].astype(F32)
            y = y * (g * jax.nn.sigmoid(g))
            y_ref[rows, sl] = jnp.where(live, y, 0.0).astype(y_ref.dtype)
        return carry

    lax.fori_loop(0, tm // GATE_ROWS, row_group, 0)


def _gate(oa, zb, yc, od, gate, na, nb, batch, tm=544):
    m = oa.shape[0]
    nt = L_PAD // tm
    row = lambda b, t: (b * nt + t, 0)
    full = lambda b, t: (0, 0)
    br = pl.BlockSpec((tm, BRANCH_W), row)
    return pl.pallas_call(
        functools.partial(_gate_kernel, tm),
        grid=(batch, nt),
        in_specs=[br, br, br, br, pl.BlockSpec((tm, D_MODEL), row),
                  pl.BlockSpec((1, BRANCH_W), full), pl.BlockSpec((1, BRANCH_W), full)],
        out_specs=pl.BlockSpec((tm, D_MODEL), row),
        out_shape=jax.ShapeDtypeStruct((m, D_MODEL), BF16),
        compiler_params=_cparams(("parallel", "parallel"), 40 << 20),
    )(oa, zb, yc, od, gate, na.reshape(1, -1), nb.reshape(1, -1))


def _inv_freq(dim, theta):
    return theta ** (-jnp.arange(0, dim, 2, dtype=F32) / dim)


def _pad_rows(t):
    return jnp.pad(t, ((0, L_PAD - t.shape[0]), (0, 0)))


def _rope_tables():
    rows = SEQ_REAL // GRID_W
    zeros_meta = jnp.zeros((N_META,), F32)
    row = jnp.concatenate([zeros_meta, jnp.repeat(jnp.arange(rows, dtype=F32), GRID_W)])
    col = jnp.concatenate([zeros_meta, jnp.tile(jnp.arange(GRID_W, dtype=F32), rows)])
    axf = _inv_freq(HEAD_DIM // 2, AXIAL_THETA)
    ar = axf[:, None] * row[None, :]
    ac = axf[:, None] * col[None, :]
    cos_a = jnp.concatenate([jnp.cos(ar), jnp.cos(ar), jnp.cos(ac), jnp.cos(ac)], axis=0)
    sin_a = jnp.concatenate([-jnp.sin(ar), jnp.sin(ar), -jnp.sin(ac), jnp.sin(ac)], axis=0)
    a1 = _inv_freq(ROPE_DIMS, ROPE_THETA)[:, None] * jnp.arange(L_TRUE, dtype=F32)[None, :]
    rest = HEAD_DIM - ROPE_DIMS
    cos_d = jnp.concatenate([jnp.cos(a1), jnp.cos(a1), jnp.ones((rest, L_TRUE), F32)], axis=0)
    sin_d = jnp.concatenate([-jnp.sin(a1), jnp.sin(a1), jnp.zeros((rest, L_TRUE), F32)], axis=0)

    def hbm_rows(tab_t):
        tab = tab_t.T
        return _pad_rows(jnp.concatenate([tab[N_META:], tab[:N_META]], axis=0))

    return hbm_rows(cos_a), hbm_rows(sin_a), hbm_rows(cos_d), hbm_rows(sin_d)


def _hyena_tables():
    t = jnp.linspace(0.0, 1.0, L_TRUE, dtype=F32)[:, None]
    bands = (HYENA_EMB_DIM - 1) // 2
    fb = jnp.linspace(1e-4, bands - 1, bands, dtype=F32)[:, None]
    wpos = 2.0 * math.pi * jnp.arange(L_TRUE, dtype=F32)[None, :] / L_TRUE
    ang = fb * wpos
    z = jnp.concatenate([t.T, jnp.cos(ang), -jnp.sin(ang)], axis=0).T
    z = jnp.pad(z, ((0, L_PAD - L_TRUE), (0, LANE - HYENA_EMB_DIM)))
    deltas = jnp.abs(jnp.linspace(math.log(HYENA_TARGET) / HYENA_SLOW,
                                  math.log(HYENA_TARGET) / HYENA_FAST, BRANCH_W, dtype=F32))
    decay = _pad_rows(jnp.exp(-t * deltas) + HYENA_SHIFT)
    return z, decay


def _pad2(a, rows, cols):
    return jnp.pad(a, ((0, rows - a.shape[0]), (0, cols - a.shape[1])))


def kernel(x, meta_tokens, norm_w, w_in, w_out, a_q_norm, a_k_norm, a_out_norm, b_short_conv,
           b_filt_w1, b_filt_b1, b_filt_w2, b_filt_b2, b_filt_w3, b_sin_freq, b_skip, b_out_norm,
           c_group_w, c_scale, d_q_norm, d_k_norm, d_lambda, d_out_norm):
    batch = x.shape[0]
    depth = norm_w.shape[0]
    m = batch * L_PAD
    tail = _pad2(meta_tokens.astype(x.dtype), L_PAD - SEQ_REAL, D_MODEL)

    cos_a, sin_a, cos_d, sin_d = _rope_tables()
    fmat, gmat = _dft_tables()
    zfeat, decay = _hyena_tables()
    bias = jnp.where(jnp.arange(L_PAD) < L_TRUE, 0.0, NEG_BIG).astype(F32).reshape(1, L_PAD)

    for l in range(depth):
        lam_init = 0.8 - 0.6 * math.exp(-0.3 * l)
        if l == 0:
            h, u = _embed_rmsnorm(x, tail, norm_w[0])
            h, u = h.reshape(m, D_MODEL), u.reshape(m, D_MODEL)
        else:
            u = _rmsnorm(h, norm_w[l])
        proj = _matmul(u, w_in, l, 0, MAIN_W, BF16)
        gate = _matmul(u, w_in, l, COL_GATE, D_MODEL, BF16)

        oa = _attn_a(proj, cos_a, sin_a, a_q_norm[l], a_k_norm[l], bias, batch)
        od = _attn_d(proj, cos_d, sin_d, d_q_norm[l], d_k_norm[l], d_lambda[l], d_out_norm[l],
                     bias, lam_init, batch)

        hsd = _hyena_filter(
            zfeat,
            _pad2(b_filt_w1[l], LANE, LANE), _pad2(b_filt_b1[l][None], 1, LANE),
            _pad2(b_filt_w2[l], LANE, LANE), _pad2(b_filt_b2[l][None], 1, LANE),
            _pad2(b_sin_freq[l][None], 1, LANE), _pad2(b_filt_w3[l], LANE, 4 * BRANCH_W), decay)
        pq = _spectra(fmat, hsd)
        z1 = _hyena_order(0, proj, COL_B, proj, b_short_conv[l], b_skip[l], fmat, gmat, pq, batch)
        z2 = _hyena_order(1, z1, 0, proj, b_short_conv[l], b_skip[l], fmat, gmat, pq, batch)

        yc = _pool(proj, c_group_w[l], c_scale[l], batch)
        y = _gate(oa, z2, yc, od, gate, a_out_norm[l], b_out_norm[l], batch)
        if l < depth - 1:
            h = _matmul(y, w_out, l, 0, D_MODEL, F32, residual=h)
        else:
            out = _matmul_final(y.reshape(batch, L_PAD, D_MODEL), w_out, l,
                                h.reshape(batch, L_PAD, D_MODEL))
    return out
```

```python
import functools
import math

import jax
import jax.numpy as jnp
from jax import lax
from jax.experimental import pallas as pl
from jax.experimental.pallas import tpu as pltpu

F32 = jnp.float32
BF16 = jnp.bfloat16

D_MODEL = 4096
N_META = 16
GRID_W = 64
EPS = 1e-6
BRANCH_W = 1024
HEAD_DIM = 128
A_HEADS = 8
A_KV_HEADS = 2
A_GROUPS = A_HEADS // A_KV_HEADS
AXIAL_THETA = 10000.0
HYENA_EMB_DIM = 33
HYENA_FFN = 64
HYENA_FAST = 0.3
HYENA_SLOW = 1.5
HYENA_TARGET = 1e-2
HYENA_SHIFT = 0.05
POOL_WINDOWS = (2, 4, 8, 16)
POOL_GROUP_W = 256
D_HEADS = 4
D_HEAD_V = 256
ROPE_THETA = 500000.0
ROPE_DIMS = 32

COL_AQ = 0
COL_AK = 1024
COL_AV = 1280
COL_B = 1536
COL_C = 4608
COL_DQ = 5632
COL_DK = 6656
COL_DV = 7680
COL_GATE = 8704
MAIN_W = COL_GATE
IN_COLS = 12800

LANE = 128
SEQ_REAL = 2048
L_TRUE = N_META + SEQ_REAL
L_PAD = 2176
DFT_N = 4607
N_FREQ = 2304
FREQ_CHUNK = 384
N_CHUNK = N_FREQ // FREQ_CHUNK
NEG_BIG = -1e30
LOG2E = 1.4426950408889634
Q_SCALE = (HEAD_DIM ** -0.5) * LOG2E
assert L_TRUE > L_PAD - LANE
VMEM_CAP = 56 * 1024 * 1024


def _cparams(semantics, vmem_bytes):
    return pltpu.CompilerParams(dimension_semantics=semantics,
                                vmem_limit_bytes=min(int(vmem_bytes), VMEM_CAP))


def _to_sequence_order(x):
    return jnp.concatenate([x[SEQ_REAL:L_TRUE], x[:SEQ_REAL], x[L_TRUE:]], axis=0)


def _store_from_sequence_order(o_ref, y, cols=slice(None)):
    o_ref[:SEQ_REAL, cols] = y[N_META:L_TRUE].astype(o_ref.dtype)
    o_ref[SEQ_REAL:L_TRUE, cols] = y[:N_META].astype(o_ref.dtype)
    o_ref[L_TRUE:, cols] = y[L_TRUE:].astype(o_ref.dtype)


def _rmsnorm_kernel(h_ref, w_ref, o_ref):
    x = h_ref[...]
    ms = jnp.mean(x * x, axis=-1, keepdims=True)
    o_ref[...] = (x * lax.rsqrt(ms + EPS) * w_ref[...]).astype(o_ref.dtype)


def _embed_rmsnorm_kernel(n_real_tiles, x_ref, tail_ref, w_ref, h_ref, u_ref):
    def emit(x):
        h_ref[...] = x
        ms = jnp.mean(x * x, axis=-1, keepdims=True)
        u_ref[...] = (x * lax.rsqrt(ms + EPS) * w_ref[...]).astype(u_ref.dtype)

    @pl.when(pl.program_id(1) < n_real_tiles)
    def _():
        emit(x_ref[...])

    @pl.when(pl.program_id(1) == n_real_tiles)
    def _():
        emit(tail_ref[...])


def _embed_rmsnorm(x, tail, w, tm=LANE):
    batch, n_real, d = x.shape
    n_real_tiles = n_real // tm
    row = lambda b, i: (b, i, 0)
    return pl.pallas_call(
        functools.partial(_embed_rmsnorm_kernel, n_real_tiles),
        grid=(batch, L_PAD // tm),
        in_specs=[pl.BlockSpec((None, tm, d), lambda b, i: (b, jnp.minimum(i, n_real_tiles - 1), 0)),
                  pl.BlockSpec((tm, d), lambda b, i: (0, 0)),
                  pl.BlockSpec((1, d), lambda b, i: (0, 0))],
        out_specs=[pl.BlockSpec((None, tm, d), row), pl.BlockSpec((None, tm, d), row)],
        out_shape=[jax.ShapeDtypeStruct((batch, L_PAD, d), F32),
                   jax.ShapeDtypeStruct((batch, L_PAD, d), BF16)],
        compiler_params=_cparams(("parallel", "arbitrary"), 32 << 20),
    )(x, tail, w.reshape(1, d))


def _rmsnorm(h, w, tm=544):
    m, d = h.shape
    return pl.pallas_call(
        _rmsnorm_kernel,
        grid=(m // tm,),
        in_specs=[pl.BlockSpec((tm, d), lambda i: (i, 0)),
                  pl.BlockSpec((1, d), lambda i: (0, 0))],
        out_specs=pl.BlockSpec((tm, d), lambda i: (i, 0)),
        out_shape=jax.ShapeDtypeStruct((m, d), BF16),
        compiler_params=_cparams(("parallel",), 40 << 20),
    )(h, w.reshape(1, d))


def _matmul_kernel(x_ref, w_ref, o_ref, wb_ref):
    @pl.when(pl.program_id(1) == 0)
    def _():
        wb_ref[...] = w_ref[...].astype(BF16)

    o_ref[...] = jnp.dot(x_ref[...], wb_ref[...],
                         preferred_element_type=F32).astype(o_ref.dtype)


def _matmul_res_kernel(x_ref, w_ref, r_ref, o_ref, wb_ref):
    @pl.when(pl.program_id(1) == 0)
    def _():
        wb_ref[...] = w_ref[...].astype(BF16)

    o_ref[...] = r_ref[...] + jnp.dot(x_ref[...], wb_ref[...], preferred_element_type=F32)


def _matmul(x, w3, layer, col0, width, out_dtype, residual=None, tm=1088, tn=512):
    m, k = x.shape
    cb0 = col0 // tn
    grid = (width // tn, m // tm)
    in_specs = [pl.BlockSpec((tm, k), lambda j, i: (i, 0)),
                pl.BlockSpec((None, k, tn), lambda j, i: (layer, 0, cb0 + j))]
    args = [x, w3]
    kern = _matmul_kernel
    if residual is not None:
        in_specs.append(pl.BlockSpec((tm, tn), lambda j, i: (i, j)))
        args.append(residual)
        kern = _matmul_res_kernel
    vmem = 2 * (tm * k * 2 + k * tn * 4 + 2 * tm * tn * 4) + k * tn * 2 + (4 << 20)
    return pl.pallas_call(
        kern,
        grid=grid,
        in_specs=in_specs,
        out_specs=pl.BlockSpec((tm, tn), lambda j, i: (i, j)),
        out_shape=jax.ShapeDtypeStruct((m, width), out_dtype),
        scratch_shapes=[pltpu.VMEM((k, tn), BF16)],
        compiler_params=_cparams(("parallel", "arbitrary"), vmem),
    )(*args)


def _matmul_final_kernel(x_ref, w_ref, r_ref, o_ref, wb_ref):
    @pl.when((pl.program_id(1) == 0) & (pl.program_id(2) == 0))
    def _():
        wb_ref[...] = w_ref[...].astype(BF16)

    o_ref[...] = r_ref[...] + jnp.dot(x_ref[...], wb_ref[...], preferred_element_type=F32)


def _matmul_final(y3, w3, layer, h3, tm=1024, tn=512):
    batch, _, k = y3.shape
    d = h3.shape[-1]
    vmem = 2 * (tm * k * 2 + k * tn * 4 + 2 * tm * tn * 4) + k * tn * 2 + (4 << 20)
    return pl.pallas_call(
        _matmul_final_kernel,
        grid=(d // tn, batch, SEQ_REAL // tm),
        in_specs=[pl.BlockSpec((None, tm, k), lambda j, b, i: (b, i, 0)),
                  pl.BlockSpec((None, k, tn), lambda j, b, i: (layer, 0, j)),
                  pl.BlockSpec((None, tm, tn), lambda j, b, i: (b, i, j))],
        out_specs=pl.BlockSpec((None, tm, tn), lambda j, b, i: (b, i, j)),
        out_shape=jax.ShapeDtypeStruct((batch, SEQ_REAL, d), F32),
        scratch_shapes=[pltpu.VMEM((k, tn), BF16)],
        compiler_params=_cparams(("parallel", "arbitrary", "arbitrary"), vmem),
    )(y3, w3, h3)


def _norm_rope(x, gain, cos, sin_signed, half_rot):
    ms = jnp.mean(x * x, axis=-1, keepdims=True)
    xn = x * lax.rsqrt(ms + EPS) * gain
    lane = lax.broadcasted_iota(jnp.int32, (1, LANE), 1)
    partner_up = (lane & half_rot) == 0
    partner = jnp.where(partner_up,
                        pltpu.roll(xn, LANE - half_rot, 1),
                        pltpu.roll(xn, half_rot, 1))
    return xn * cos + partner * sin_signed


def _softmax_pv(q_bf, k_bf, v_bf, bias_last, denom_from_ones):
    s = lax.dot_general(q_bf, k_bf, (((1,), (1,)), ((), ())), preferred_element_type=F32)
    n_main = L_PAD - LANE
    s_main = s[:, :n_main]
    s_last = s[:, n_main:] + bias_last
    m = jnp.maximum(jnp.max(s_main, axis=-1, keepdims=True),
                    jnp.max(s_last, axis=-1, keepdims=True))
    p_main = jnp.exp2(s_main - m)
    p_last = jnp.exp2(s_last - m)
    p = jnp.concatenate([p_main.astype(BF16), p_last.astype(BF16)], axis=1)
    o = jnp.dot(p, v_bf, preferred_element_type=F32)
    if denom_from_ones:
        return o[:, :HEAD_DIM] * (1.0 / o[:, HEAD_DIM:HEAD_DIM + 1])
    l = jnp.sum(p_main, axis=-1, keepdims=True) + jnp.sum(p_last, axis=-1, keepdims=True)
    return o * (1.0 / l)


def _attn_a_kernel(sub, q_ref, k_ref, v_ref, cq_ref, sq_ref, ck_ref, sk_ref, gq_ref, gk_ref,
                   bias_ref, o_ref, kh_ref, vx_ref, qh_ref):
    @pl.when((pl.program_id(2) == 0) & (pl.program_id(3) == 0))
    def _():
        kh_ref[...] = _norm_rope(k_ref[...].astype(F32), gk_ref[...], ck_ref[...], sk_ref[...],
                                 HEAD_DIM // 4).astype(BF16)
        vx_ref[:, :HEAD_DIM] = v_ref[...]
        vx_ref[:, HEAD_DIM:] = jnp.ones((L_PAD, HEAD_DIM), BF16)

    q = _norm_rope(q_ref[...].astype(F32), gq_ref[...], cq_ref[...], sq_ref[...], HEAD_DIM // 4)
    qh_ref[...] = (q * Q_SCALE).astype(BF16)
    bias_last = bias_ref[:, L_PAD - LANE:]
    for i in range(q_ref.shape[0] // sub):
        rows = slice(i * sub, (i + 1) * sub)
        o_ref[rows, :] = _softmax_pv(qh_ref[rows, :], kh_ref[...], vx_ref[...], bias_last,
                                     True).astype(o_ref.dtype)


def _attn_a(proj, cos_t, sin_t, gq, gk, bias, batch, tq=L_PAD, sub=272):
    m = proj.shape[0]
    nq = L_PAD // tq
    qblk = lambda b, kv, g, qi: (b * nq + qi, COL_AQ // HEAD_DIM + kv * A_GROUPS + g)
    tabq = lambda b, kv, g, qi: (qi, 0)
    full = lambda b, kv, g, qi: (0, 0)
    return pl.pallas_call(
        functools.partial(_attn_a_kernel, sub),
        grid=(batch, A_KV_HEADS, A_GROUPS, nq),
        in_specs=[
            pl.BlockSpec((tq, HEAD_DIM), qblk),
            pl.BlockSpec((L_PAD, HEAD_DIM), lambda b, kv, g, qi: (b, COL_AK // HEAD_DIM + kv)),
            pl.BlockSpec((L_PAD, HEAD_DIM), lambda b, kv, g, qi: (b, COL_AV // HEAD_DIM + kv)),
            pl.BlockSpec((tq, HEAD_DIM), tabq),
            pl.BlockSpec((tq, HEAD_DIM), tabq),
            pl.BlockSpec((L_PAD, HEAD_DIM), full),
            pl.BlockSpec((L_PAD, HEAD_DIM), full),
            pl.BlockSpec((1, HEAD_DIM), full),
            pl.BlockSpec((1, HEAD_DIM), full),
            pl.BlockSpec((1, L_PAD), full),
        ],
        out_specs=pl.BlockSpec((tq, HEAD_DIM), lambda b, kv, g, qi: (b * nq + qi, kv * A_GROUPS + g)),
        out_shape=jax.ShapeDtypeStruct((m, BRANCH_W), BF16),
        scratch_shapes=[pltpu.VMEM((L_PAD, HEAD_DIM), BF16),
                        pltpu.VMEM((L_PAD, 2 * HEAD_DIM), BF16),
                        pltpu.VMEM((tq, HEAD_DIM), BF16)],
        compiler_params=_cparams(("parallel", "parallel", "arbitrary", "arbitrary"), 48 << 20),
    )(proj, proj, proj, cos_t, sin_t, cos_t, sin_t, gq.reshape(1, -1), gk.reshape(1, -1), bias)


def _attn_d_kernel(lam_init, sub, q_ref, k_ref, v_ref, cq_ref, sq_ref, ck_ref, sk_ref, gq_ref,
                   gk_ref, lam_ref, gout_ref, bias_ref, o_ref, kh_ref, qh_ref):
    halves = (slice(0, HEAD_DIM), slice(HEAD_DIM, 2 * HEAD_DIM))

    @pl.when(pl.program_id(2) == 0)
    def _():
        for sl in halves:
            kh_ref[:, sl] = _norm_rope(k_ref[:, sl].astype(F32), gk_ref[...], ck_ref[...],
                                       sk_ref[...], ROPE_DIMS // 2).astype(BF16)

    for sl in halves:
        q = _norm_rope(q_ref[:, sl].astype(F32), gq_ref[...], cq_ref[...], sq_ref[...],
                       ROPE_DIMS // 2)
        qh_ref[:, sl] = (q * Q_SCALE).astype(BF16)
    lf = lam_ref[...]
    lam = (jnp.exp(jnp.sum(lf[0:1] * lf[1:2], axis=-1, keepdims=True))
           - jnp.exp(jnp.sum(lf[2:3] * lf[3:4], axis=-1, keepdims=True)) + lam_init)
    bias_last = bias_ref[:, L_PAD - LANE:]
    for i in range(q_ref.shape[0] // sub):
        rows = slice(i * sub, (i + 1) * sub)
        o1, o2 = [_softmax_pv(qh_ref[rows, sl], kh_ref[:, sl], v_ref[...], bias_last, False)
                  for sl in halves]
        o = o1 - lam * o2
        ms = jnp.mean(o * o, axis=-1, keepdims=True)
        o = o * lax.rsqrt(ms + EPS) * gout_ref[...] * (1.0 - lam_init)
        o_ref[rows, :] = o.astype(o_ref.dtype)


def _attn_d(proj, cos_t, sin_t, gq, gk, lam_vec, gout, bias, lam_init, batch, tq=1088, sub=272):
    m = proj.shape[0]
    nq = L_PAD // tq
    w = 2 * HEAD_DIM
    tabq = lambda b, h, qi: (qi, 0)
    full = lambda b, h, qi: (0, 0)
    return pl.pallas_call(
        functools.partial(_attn_d_kernel, lam_init, sub),
        grid=(batch, D_HEADS, nq),
        in_specs=[
            pl.BlockSpec((tq, w), lambda b, h, qi: (b * nq + qi, COL_DQ // w + h)),
            pl.BlockSpec((L_PAD, w), lambda b, h, qi: (b, COL_DK // w + h)),
            pl.BlockSpec((L_PAD, w), lambda b, h, qi: (b, COL_DV // w + h)),
            pl.BlockSpec((tq, HEAD_DIM), tabq),
            pl.BlockSpec((tq, HEAD_DIM), tabq),
            pl.BlockSpec((L_PAD, HEAD_DIM), full),
            pl.BlockSpec((L_PAD, HEAD_DIM), full),
            pl.BlockSpec((1, HEAD_DIM), full),
            pl.BlockSpec((1, HEAD_DIM), full),
            pl.BlockSpec((4, HEAD_DIM), full),
            pl.BlockSpec((1, w), full),
            pl.BlockSpec((1, L_PAD), full),
        ],
        out_specs=pl.BlockSpec((tq, w), lambda b, h, qi: (b * nq + qi, h)),
        out_shape=jax.ShapeDtypeStruct((m, BRANCH_W), BF16),
        scratch_shapes=[pltpu.VMEM((L_PAD, w), BF16), pltpu.VMEM((tq, w), BF16)],
        compiler_params=_cparams(("parallel", "parallel", "arbitrary"), 48 << 20),
    )(proj, proj, proj, cos_t, sin_t, cos_t, sin_t, gq.reshape(1, -1), gk.reshape(1, -1),
      lam_vec, gout.reshape(1, -1), bias)


def _hyena_filter_kernel(z_ref, w1_ref, b1_ref, w2_ref, b2_ref, fr_ref, w3f_ref, w3b_ref,
                         dec_ref, o_ref, hid_ref):
    hp = lax.Precision.HIGHEST

    @pl.when((pl.program_id(0) == 0) & (pl.program_id(1) == 0))
    def _():
        fr = fr_ref[...]
        h1 = jnp.sin(fr * (jnp.dot(z_ref[...], w1_ref[...], precision=hp,
                                   preferred_element_type=F32) + b1_ref[...]))
        hid_ref[...] = jnp.sin(fr * (jnp.dot(h1, w2_ref[...], precision=hp,
                                             preferred_element_type=F32) + b2_ref[...]))

    hid = hid_ref[...]
    dec = dec_ref[...]
    lag = lax.broadcasted_iota(jnp.int32, (L_PAD, 1), 0)
    hid_bf = hid.astype(BF16)
    hf = jnp.dot(hid_bf, w3f_ref[...].astype(BF16), preferred_element_type=F32) * dec
    hb = jnp.dot(hid_bf, w3b_ref[...].astype(BF16), preferred_element_type=F32) * dec
    hf = jnp.where(lag < L_TRUE, hf, 0.0)
    hb = jnp.where((lag >= 1) & (lag < L_TRUE), hb, 0.0)
    o_ref[0] = (hf + hb).astype(o_ref.dtype)
    o_ref[1] = (hf - hb).astype(o_ref.dtype)


def _hyena_filter(zfeat, w1, b1, w2, b2, fr, w3, decay, tc=512):
    nct = BRANCH_W // tc
    full = lambda o, c: (0, 0)
    return pl.pallas_call(
        _hyena_filter_kernel,
        grid=(2, nct),
        in_specs=[
            pl.BlockSpec((L_PAD, LANE), full),
            pl.BlockSpec((LANE, LANE), full),
            pl.BlockSpec((1, LANE), full),
            pl.BlockSpec((LANE, LANE), full),
            pl.BlockSpec((1, LANE), full),
            pl.BlockSpec((1, LANE), full),
            pl.BlockSpec((LANE, tc), lambda o, c: (0, (o * 2) * nct + c)),
            pl.BlockSpec((LANE, tc), lambda o, c: (0, (o * 2 + 1) * nct + c)),
            pl.BlockSpec((L_PAD, tc), lambda o, c: (0, c)),
        ],
        out_specs=pl.BlockSpec((2, L_PAD, tc), lambda o, c: (0, 0, o * nct + c)),
        out_shape=jax.ShapeDtypeStruct((2, L_PAD, 2 * BRANCH_W), BF16),
        scratch_shapes=[pltpu.VMEM((L_PAD, LANE), F32)],
        compiler_params=_cparams(("arbitrary", "arbitrary"), 48 << 20),
    )(zfeat, w1, b1, w2, b2, fr, w3, w3, decay)


def _dft_tables_kernel(bc_ref, bs_ref, btc_ref, bts_ref, sc_ref, ss_ref, sct_ref, sst_ref,
                       f_ref, g_ref):
    sc, ss = sc_ref[...], ss_ref[...]
    bc, bs = bc_ref[...], bs_ref[...]
    f_ref[:FREQ_CHUNK, :] = (sc * bc - ss * bs).astype(f_ref.dtype)
    f_ref[FREQ_CHUNK:, :] = (ss * bc + sc * bs).astype(f_ref.dtype)
    sct, sst = sct_ref[...], sst_ref[...]
    btc, bts = btc_ref[...], bts_ref[...]
    g_ref[:, :FREQ_CHUNK] = (sct * btc - sst * bts).astype(g_ref.dtype)
    g_ref[:, FREQ_CHUNK:] = (sst * btc + sct * bts).astype(g_ref.dtype)


def _dft_tables():
    def trig(k, t):
        ang = ((k * t) % DFT_N).astype(F32) * (2.0 * math.pi / DFT_N)
        return jnp.cos(ang), jnp.sin(ang)

    t = jnp.arange(L_PAD, dtype=jnp.int32)
    fine = jnp.arange(16, dtype=jnp.int32)
    coarse = jnp.arange(FREQ_CHUNK // 16, dtype=jnp.int32) * 16
    fc, fs = trig(fine[:, None], t[None, :])
    cc, cs = trig(coarse[:, None], t[None, :])
    bc = (cc[:, None] * fc[None] - cs[:, None] * fs[None]).reshape(FREQ_CHUNK, L_PAD)
    bs = (cs[:, None] * fc[None] + cc[:, None] * fs[None]).reshape(FREQ_CHUNK, L_PAD)
    live = (t < L_TRUE).astype(F32)[None, :]
    sc, ss = trig((jnp.arange(N_CHUNK, dtype=jnp.int32) * FREQ_CHUNK)[:, None], t[None, :])
    sc, ss = sc * live, ss * live
    full2 = lambda j: (0, 0)
    step_row = pl.BlockSpec((None, 1, L_PAD), lambda j: (j, 0, 0))
    step_col = pl.BlockSpec((None, L_PAD, 1), lambda j: (j, 0, 0))
    return pl.pallas_call(
        _dft_tables_kernel,
        grid=(N_CHUNK,),
        in_specs=[pl.BlockSpec((FREQ_CHUNK, L_PAD), full2), pl.BlockSpec((FREQ_CHUNK, L_PAD), full2),
                  pl.BlockSpec((L_PAD, FREQ_CHUNK), full2), pl.BlockSpec((L_PAD, FREQ_CHUNK), full2),
                  step_row, step_row, step_col, step_col],
        out_specs=[pl.BlockSpec((None, 2 * FREQ_CHUNK, L_PAD), lambda j: (j, 0, 0)),
                   pl.BlockSpec((None, L_PAD, 2 * FREQ_CHUNK), lambda j: (j, 0, 0))],
        out_shape=[jax.ShapeDtypeStruct((N_CHUNK, 2 * FREQ_CHUNK, L_PAD), BF16),
                   jax.ShapeDtypeStruct((N_CHUNK, L_PAD, 2 * FREQ_CHUNK), BF16)],
        compiler_params=_cparams(("parallel",), 48 << 20),
    )(bc, bs, bc.T, bs.T, sc[:, None, :], ss[:, None, :], sc[:, :, None], ss[:, :, None])


def _spectra_kernel(f_ref, h_ref, o_ref):
    k = pl.program_id(1) * FREQ_CHUNK + lax.broadcasted_iota(jnp.int32, (FREQ_CHUNK, 1), 0)
    wk = jnp.where(k == 0, 1.0 / DFT_N, 2.0 / DFT_N)
    o_ref[:FREQ_CHUNK, :] = jnp.dot(f_ref[:FREQ_CHUNK, :], h_ref[0],
                                    preferred_element_type=F32) * wk
    o_ref[FREQ_CHUNK:, :] = jnp.dot(f_ref[FREQ_CHUNK:, :], h_ref[1],
                                    preferred_element_type=F32) * wk


def _spectra(fmat, hsd, tn=1024):
    ncol = hsd.shape[-1]
    return pl.pallas_call(
        _spectra_kernel,
        grid=(ncol // tn, N_CHUNK),
        in_specs=[pl.BlockSpec((None, 2 * FREQ_CHUNK, L_PAD), lambda n, f: (f, 0, 0)),
                  pl.BlockSpec((2, L_PAD, tn), lambda n, f: (0, 0, n))],
        out_specs=pl.BlockSpec((None, 2 * FREQ_CHUNK, tn), lambda n, f: (f, 0, n)),
        out_shape=jax.ShapeDtypeStruct((N_CHUNK, 2 * FREQ_CHUNK, ncol), F32),
        compiler_params=_cparams(("parallel", "arbitrary"), 40 << 20),
    )(fmat, hsd)


def _short_conv3(x, w):
    return (pltpu.roll(x, 1, 0) * w[0:1] + x * w[1:2] + pltpu.roll(x, L_PAD - 1, 0) * w[2:3])


def _hyena_order_kernel(first, z_ref, g_ref, wz_ref, wg_ref, skip_ref, f_ref, gm_ref, pq_ref,
                        o_ref, zs_ref, zf_ref, acc_ref):
    fk = pl.program_id(2)

    @pl.when(fk == 0)
    def _():
        z = z_ref[...].astype(F32)
        if first:
            z = _short_conv3(_to_sequence_order(z), wz_ref[...])
        zf_ref[...] = z
        zs_ref[...] = z.astype(BF16)
        acc_ref[...] = jnp.zeros_like(acc_ref)

    ab = jnp.dot(f_ref[...], zs_ref[...], preferred_element_type=F32)
    a, b = ab[:FREQ_CHUNK], ab[FREQ_CHUNK:]
    p, q = pq_ref[:FREQ_CHUNK, :], pq_ref[FREQ_CHUNK:, :]
    uv = jnp.concatenate([(a * p - b * q).astype(BF16), (a * q + b * p).astype(BF16)], axis=0)
    acc_ref[...] += jnp.dot(gm_ref[...], uv, preferred_element_type=F32)

    @pl.when(fk == pl.num_programs(2) - 1)
    def _():
        gate = _short_conv3(_to_sequence_order(g_ref[...].astype(F32)), wg_ref[...])
        y = gate * (acc_ref[...] + skip_ref[...] * zf_ref[...])
        if first:
            o_ref[...] = y
        else:
            _store_from_sequence_order(o_ref, y)


def _hyena_order(order, zin, zin_col0, proj, conv_w, skip, fmat, gmat, pq, batch, tc=512):
    first = order == 0
    m = proj.shape[0]
    nct = BRANCH_W // tc
    zcb = zin_col0 // tc
    gcb = (COL_B + (order + 1) * BRANCH_W) // tc
    in_specs = [
        pl.BlockSpec((L_PAD, tc), lambda b, c, f: (b, zcb + c)),
        pl.BlockSpec((L_PAD, tc), lambda b, c, f: (b, gcb + c)),
        pl.BlockSpec((3, tc), lambda b, c, f: (0, c)),
        pl.BlockSpec((3, tc), lambda b, c, f: (0, (order + 1) * nct + c)),
        pl.BlockSpec((1, tc), lambda b, c, f: (0, c)),
        pl.BlockSpec((None, 2 * FREQ_CHUNK, L_PAD), lambda b, c, f: (f, 0, 0)),
        pl.BlockSpec((None, L_PAD, 2 * FREQ_CHUNK), lambda b, c, f: (f, 0, 0)),
        pl.BlockSpec((None, 2 * FREQ_CHUNK, tc), lambda b, c, f: (f, 0, order * nct + c)),
    ]
    return pl.pallas_call(
        functools.partial(_hyena_order_kernel, first),
        grid=(batch, nct, N_CHUNK),
        in_specs=in_specs,
        out_specs=pl.BlockSpec((L_PAD, tc), lambda b, c, f: (b, c)),
        out_shape=jax.ShapeDtypeStruct((m, BRANCH_W), F32 if first else BF16),
        scratch_shapes=[pltpu.VMEM((L_PAD, tc), BF16),
                        pltpu.VMEM((L_PAD, tc), F32),
                        pltpu.VMEM((L_PAD, tc), F32)],
        compiler_params=_cparams(("parallel", "parallel", "arbitrary"), VMEM_CAP),
    )(zin, proj, conv_w, conv_w, skip[order].reshape(1, BRANCH_W), fmat, gmat, pq)


def _pool_kernel(p0_ref, p1_ref, p2_ref, p3_ref, gw_ref, scale_ref, o_ref):
    t = lax.broadcasted_iota(jnp.int32, (L_PAD, 1), 0)
    for g, (w, p_ref) in enumerate(zip(POOL_WINDOWS, (p0_ref, p1_ref, p2_ref, p3_ref))):
        p = _to_sequence_order(p_ref[...].astype(F32))
        s = p
        span = 1
        while span < w:
            s = s + pltpu.roll(s, L_PAD - span, 0)
            span *= 2
        back = (w - 1) // 2
        if back:
            s = pltpu.roll(s, back, 0)
        lo = jnp.clip(t - back, 0, L_TRUE)
        hi = jnp.clip(t + w // 2 + 1, 0, L_TRUE)
        cnt = jnp.maximum(hi - lo, 1).astype(F32)
        d = (s / cnt - p).astype(BF16)
        y = jnp.dot(d, gw_ref[g].astype(BF16), preferred_element_type=F32)
        sl = slice(g * POOL_GROUP_W, (g + 1) * POOL_GROUP_W)
        _store_from_sequence_order(o_ref, y * scale_ref[:, sl], sl)


def _pool(proj, group_w, scale, batch):
    m = proj.shape[0]
    cb = COL_C // POOL_GROUP_W
    grp = lambda g: pl.BlockSpec((L_PAD, POOL_GROUP_W), lambda b: (b, cb + g))
    return pl.pallas_call(
        _pool_kernel,
        grid=(batch,),
        in_specs=[grp(0), grp(1), grp(2), grp(3),
                  pl.BlockSpec((4, POOL_GROUP_W, POOL_GROUP_W), lambda b: (0, 0, 0)),
                  pl.BlockSpec((1, BRANCH_W), lambda b: (0, 0))],
        out_specs=pl.BlockSpec((L_PAD, BRANCH_W), lambda b: (b, 0)),
        out_shape=jax.ShapeDtypeStruct((m, BRANCH_W), BF16),
        compiler_params=_cparams(("parallel",), 48 << 20),
    )(proj, proj, proj, proj, group_w, scale.reshape(1, -1))


GATE_ROWS = 16


def _gate_kernel(tm, oa_ref, zb_ref, yc_ref, od_ref, gate_ref, na_ref, nb_ref, y_ref):
    branches = ((oa_ref, na_ref), (zb_ref, nb_ref), (yc_ref, None), (od_ref, None))

    def row_group(r, carry):
        row0 = pl.multiple_of(r * GATE_ROWS, GATE_ROWS)
        rows = pl.ds(row0, GATE_ROWS)
        pos = (pl.program_id(1) * tm + row0
               + lax.broadcasted_iota(jnp.int32, (GATE_ROWS, 1), 0))
        live = pos < L_TRUE
        for i, (x_ref, norm_ref) in enumerate(branches):
            sl = slice(i * BRANCH_W, (i + 1) * BRANCH_W)
            y = x_ref[rows, :].astype(F32)
            if norm_ref is not None:
                ms = jnp.mean(y * y, axis=-1, keepdims=True)
                y = y * lax.rsqrt(ms + EPS) * norm_ref[...]
            g = gate_ref[rows, sl].astype(F32)
            y = y * (g * jax.nn.sigmoid(g))
            y_ref[rows, sl] = jnp.where(live, y, 0.0).astype(y_ref.dtype)
        return carry

    lax.fori_loop(0, tm // GATE_ROWS, row_group, 0)


def _gate(oa, zb, yc, od, gate, na, nb, batch, tm=544):
    m = oa.shape[0]
    nt = L_PAD // tm
    row = lambda b, t: (b * nt + t, 0)
    full = lambda b, t: (0, 0)
    br = pl.BlockSpec((tm, BRANCH_W), row)
    return pl.pallas_call(
        functools.partial(_gate_kernel, tm),
        grid=(batch, nt),
        in_specs=[br, br, br, br, pl.BlockSpec((tm, D_MODEL), row),
                  pl.BlockSpec((1, BRANCH_W), full), pl.BlockSpec((1, BRANCH_W), full)],
        out_specs=pl.BlockSpec((tm, D_MODEL), row),
        out_shape=jax.ShapeDtypeStruct((m, D_MODEL), BF16),
        compiler_params=_cparams(("parallel", "parallel"), 40 << 20),
    )(oa, zb, yc, od, gate, na.reshape(1, -1), nb.reshape(1, -1))


def _inv_freq(dim, theta):
    return theta ** (-jnp.arange(0, dim, 2, dtype=F32) / dim)


def _pad_rows(t):
    return jnp.pad(t, ((0, L_PAD - t.shape[0]), (0, 0)))


def _rope_tables():
    rows = SEQ_REAL // GRID_W
    zeros_meta = jnp.zeros((N_META,), F32)
    row = jnp.concatenate([zeros_meta, jnp.repeat(jnp.arange(rows, dtype=F32), GRID_W)])
    col = jnp.concatenate([zeros_meta, jnp.tile(jnp.arange(GRID_W, dtype=F32), rows)])
    axf = _inv_freq(HEAD_DIM // 2, AXIAL_THETA)
    ar = axf[:, None] * row[None, :]
    ac = axf[:, None] * col[None, :]
    cos_a = jnp.concatenate([jnp.cos(ar), jnp.cos(ar), jnp.cos(ac), jnp.cos(ac)], axis=0)
    sin_a = jnp.concatenate([-jnp.sin(ar), jnp.sin(ar), -jnp.sin(ac), jnp.sin(ac)], axis=0)
    a1 = _inv_freq(ROPE_DIMS, ROPE_THETA)[:, None] * jnp.arange(L_TRUE, dtype=F32)[None, :]
    rest = HEAD_DIM - ROPE_DIMS
    cos_d = jnp.concatenate([jnp.cos(a1), jnp.cos(a1), jnp.ones((rest, L_TRUE), F32)], axis=0)
    sin_d = jnp.concatenate([-jnp.sin(a1), jnp.sin(a1), jnp.zeros((rest, L_TRUE), F32)], axis=0)

    def hbm_rows(tab_t):
        tab = tab_t.T
        return _pad_rows(jnp.concatenate([tab[N_META:], tab[:N_META]], axis=0))

    return hbm_rows(cos_a), hbm_rows(sin_a), hbm_rows(cos_d), hbm_rows(sin_d)


def _hyena_tables():
    t = jnp.linspace(0.0, 1.0, L_TRUE, dtype=F32)[:, None]
    bands = (HYENA_EMB_DIM - 1) // 2
    fb = jnp.linspace(1e-4, bands - 1, bands, dtype=F32)[:, None]
    wpos = 2.0 * math.pi * jnp.arange(L_TRUE, dtype=F32)[None, :] / L_TRUE
    ang = fb * wpos
    z = jnp.concatenate([t.T, jnp.cos(ang), -jnp.sin(ang)], axis=0).T
    z = jnp.pad(z, ((0, L_PAD - L_TRUE), (0, LANE - HYENA_EMB_DIM)))
    deltas = jnp.abs(jnp.linspace(math.log(HYENA_TARGET) / HYENA_SLOW,
                                  math.log(HYENA_TARGET) / HYENA_FAST, BRANCH_W, dtype=F32))
    decay = _pad_rows(jnp.exp(-t * deltas) + HYENA_SHIFT)
    return z, decay


def _pad2(a, rows, cols):
    return jnp.pad(a, ((0, rows - a.shape[0]), (0, cols - a.shape[1])))


def kernel(x, meta_tokens, norm_w, w_in, w_out, a_q_norm, a_k_norm, a_out_norm, b_short_conv,
           b_filt_w1, b_filt_b1, b_filt_w2, b_filt_b2, b_filt_w3, b_sin_freq, b_skip, b_out_norm,
           c_group_w, c_scale, d_q_norm, d_k_norm, d_lambda, d_out_norm):
    batch = x.shape[0]
    depth = norm_w.shape[0]
    m = batch * L_PAD
    tail = _pad2(meta_tokens.astype(x.dtype), L_PAD - SEQ_REAL, D_MODEL)

    cos_a, sin_a, cos_d, sin_d = _rope_tables()
    fmat, gmat = _dft_tables()
    zfeat, decay = _hyena_tables()
    bias = jnp.where(jnp.arange(L_PAD) < L_TRUE, 0.0, NEG_BIG).astype(F32).reshape(1, L_PAD)

    for l in range(depth):
        lam_init = 0.8 - 0.6 * math.exp(-0.3 * l)
        if l == 0:
            h, u = _embed_rmsnorm(x, tail, norm_w[0])
            h, u = h.reshape(m, D_MODEL), u.reshape(m, D_MODEL)
        else:
            u = _rmsnorm(h, norm_w[l])
        proj = _matmul(u, w_in, l, 0, MAIN_W, BF16)
        gate = _matmul(u, w_in, l, COL_GATE, D_MODEL, BF16)

        oa = _attn_a(proj, cos_a, sin_a, a_q_norm[l], a_k_norm[l], bias, batch)
        od = _attn_d(proj, cos_d, sin_d, d_q_norm[l], d_k_norm[l], d_lambda[l], d_out_norm[l],
                     bias, lam_init, batch)

        hsd = _hyena_filter(
            zfeat,
            _pad2(b_filt_w1[l], LANE, LANE), _pad2(b_filt_b1[l][None], 1, LANE),
            _pad2(b_filt_w2[l], LANE, LANE), _pad2(b_filt_b2[l][None], 1, LANE),
            _pad2(b_sin_freq[l][None], 1, LANE), _pad2(b_filt_w3[l], LANE, 4 * BRANCH_W), decay)
        pq = _spectra(fmat, hsd)
        z1 = _hyena_order(0, proj, COL_B, proj, b_short_conv[l], b_skip[l], fmat, gmat, pq, batch)
        z2 = _hyena_order(1, z1, 0, proj, b_short_conv[l], b_skip[l], fmat, gmat, pq, batch)

        yc = _pool(proj, c_group_w[l], c_scale[l], batch)
        y = _gate(oa, z2, yc, od, gate, a_out_norm[l], b_out_norm[l], batch)
        if l < depth - 1:
            h = _matmul(y, w_out, l, 0, D_MODEL, F32, residual=h)
        else:
            out = _matmul_final(y.reshape(batch, L_PAD, D_MODEL), w_out, l,
                                h.reshape(batch, L_PAD, D_MODEL))
    return out
```
